```python
import jax, jax.numpy as jnp
from jax import lax
import numpy as np

D_MODEL = 2048
BATCH = 2
SEQ = 8192
DEPTH = 2
DEC_BATCH = 8
DEC_SEQ = 32
PAST_LEN = 4096

CHUNK = 64
Q_BLOCK = 128
FOX_HEAD_DIM = 128
FOX_WIDTH = D_MODEL // 2
FOX_HEADS = FOX_WIDTH // FOX_HEAD_DIM
GLA_HEADS = 4
GLA_KEY_WIDTH = D_MODEL // 2
GLA_VAL_WIDTH = D_MODEL // 2
GLA_DK = GLA_KEY_WIDTH // GLA_HEADS
GLA_DV = GLA_VAL_WIDTH // GLA_HEADS
GLA_GATE_RANK = 16
GLA_GATE_TAU = 16.0
NORM_EPS = 1e-6
IN_SIZES = (FOX_WIDTH, FOX_WIDTH, FOX_WIDTH, FOX_HEADS, FOX_WIDTH,
            GLA_KEY_WIDTH, GLA_KEY_WIDTH, GLA_VAL_WIDTH, GLA_GATE_RANK, GLA_VAL_WIDTH,
            D_MODEL, D_MODEL)
IN_WIDTH = 4 * FOX_WIDTH + FOX_HEADS + 2 * GLA_KEY_WIDTH + 2 * GLA_VAL_WIDTH + GLA_GATE_RANK + 2 * D_MODEL

kernel_name = "fox_gla_gated_hybrid_stream_step"


def rmsnorm(x, g):
    xf = x.astype(jnp.float32)
    y = xf * lax.rsqrt(jnp.mean(xf * xf, axis=-1, keepdims=True) + NORM_EPS)
    return (y * g.astype(jnp.float32)).astype(x.dtype)


def fox_attention(q, k, v, cq, ck):
    B, Lq, H, dh = q.shape
    Lk = k.shape[1]
    qb = Q_BLOCK if Lq % Q_BLOCK == 0 else Lq
    nb = Lq // qb
    kpos = jnp.arange(Lk)
    qpos = jnp.arange(Lq) + (Lk - Lq)
    scale = dh ** -0.5
    ck_t = jnp.transpose(ck, (0, 2, 1))[:, :, None, :]

    def block(args):
        qi, cqi, pi = args
        s = jnp.einsum('bqhd,bkhd->bhqk', qi, k, preferred_element_type=jnp.float32) * scale
        s = s + jnp.transpose(cqi, (0, 2, 1))[..., None] - ck_t
        s = jnp.where(kpos[None, :] <= pi[:, None], s, -jnp.inf)
        p = jax.nn.softmax(s, axis=-1)
        return jnp.einsum('bhqk,bkhd->bqhd', p.astype(v.dtype), v)

    qs = q.reshape(B, nb, qb, H, dh).transpose(1, 0, 2, 3, 4)
    cs = cq.reshape(B, nb, qb, H).transpose(1, 0, 2, 3)
    ps = qpos.reshape(nb, qb)
    out = lax.map(block, (qs, cs, ps))
    return out.transpose(1, 0, 2, 3, 4).reshape(B, Lq, H, dh)


def gla_chunked(q, k, v, log_a, s0):
    B, L, H, dk = q.shape
    dv = v.shape[-1]
    c = CHUNK if L % CHUNK == 0 else L
    n = L // c

    def to_chunks(t):
        return t.astype(jnp.float32).reshape(B, n, c, H, t.shape[-1]).transpose(1, 0, 2, 3, 4)

    tril = jnp.tril(jnp.ones((c, c), dtype=bool))

    def step(S, xs):
        qc, kc, vc, gc = xs
        b = jnp.cumsum(gc, axis=1)
        diff = b[:, :, None] - b[:, None, :]
        decay = jnp.exp(jnp.where(tril[None, :, :, None, None], diff, -jnp.inf))
        a = jnp.einsum('btshk,bshk->bhts', qc[:, :, None] * decay, kc)
        o = jnp.einsum('bhts,bshv->bthv', a, vc) + jnp.einsum('bthk,bhkv->bthv', qc * jnp.exp(b), S)
        b_last = b[:, -1]
        S = jnp.exp(b_last)[..., None] * S + jnp.einsum('bshk,bshv->bhkv', kc * jnp.exp(b_last[:, None] - b), vc)
        return S, o

    S, o = lax.scan(step, s0.astype(jnp.float32), (to_chunks(q), to_chunks(k), to_chunks(v), to_chunks(log_a)))
    return S, o.transpose(1, 0, 2, 3, 4).reshape(B, L, H, dv)


def mixer_layer(x, past_k, past_v, past_logf, s0, w_in, w_a2, b_a, b_f, gla_gain, w_oa, w_ob, w_out, pre_g, post_g):
    B, L, _ = x.shape
    h = rmsnorm(x, pre_g)
    proj = h @ w_in
    idx = [int(i) for i in np.cumsum(IN_SIZES)[:-1]]
    fq, fk, fv, ff, fg, gq, gk, gv, glr, gg, ma, mb = jnp.split(proj, idx, axis=-1)

    fq = fq.reshape(B, L, FOX_HEADS, FOX_HEAD_DIM)
    fk = fk.reshape(B, L, FOX_HEADS, FOX_HEAD_DIM)
    fv = fv.reshape(B, L, FOX_HEADS, FOX_HEAD_DIM)
    logf = jax.nn.log_sigmoid((ff + b_f).astype(jnp.float32))
    k_all = jnp.concatenate([past_k.astype(fk.dtype), fk], axis=1)
    v_all = jnp.concatenate([past_v.astype(fv.dtype), fv], axis=1)
    c_all = jnp.cumsum(jnp.concatenate([past_logf.astype(jnp.float32), logf], axis=1), axis=1)
    o_a = fox_attention(fq, k_all, v_all, c_all[:, -L:], c_all)
    o_a = o_a.reshape(B, L, FOX_WIDTH) * jax.nn.silu(fg)

    gq = gq.reshape(B, L, GLA_HEADS, GLA_DK) * (GLA_DK ** -0.5)
    gk = gk.reshape(B, L, GLA_HEADS, GLA_DK)
    gv = gv.reshape(B, L, GLA_HEADS, GLA_DV)
    log_a = jax.nn.log_sigmoid((glr @ w_a2 + b_a).astype(jnp.float32)) / GLA_GATE_TAU
    log_a = log_a.reshape(B, L, GLA_HEADS, GLA_DK)
    s_new, o_b = gla_chunked(gq, gk, gv, log_a, s0)
    o_b = o_b * lax.rsqrt(jnp.mean(o_b * o_b, axis=-1, keepdims=True) + NORM_EPS) * gla_gain.astype(jnp.float32)
    o_b = o_b.reshape(B, L, GLA_VAL_WIDTH).astype(x.dtype) * jax.nn.silu(gg)

    merged = jax.nn.sigmoid(ma) * (o_a @ w_oa) + jax.nn.sigmoid(mb) * (o_b @ w_ob)
    y = x + rmsnorm(merged @ w_out, post_g)
    return y, fk, fv, logf, s_new


def setup_inputs(seed: int = 0) -> dict:
    key = jax.random.key(seed)
    ks = jax.random.split(key, 20)
    f32 = jnp.float32
    n = jax.random.normal
    return {
        "x_prompt": n(ks[0], (BATCH, SEQ, D_MODEL), f32),
        "x_sample": n(ks[1], (DEC_BATCH, DEC_SEQ, D_MODEL), f32),
        "cache_k": n(ks[2], (DEPTH, DEC_BATCH, PAST_LEN, FOX_HEADS, FOX_HEAD_DIM), f32),
        "cache_v": n(ks[3], (DEPTH, DEC_BATCH, PAST_LEN, FOX_HEADS, FOX_HEAD_DIM), f32),
        "cache_logf": jax.nn.log_sigmoid(2.0 + 0.5 * n(ks[4], (DEPTH, DEC_BATCH, PAST_LEN, FOX_HEADS), f32)),
        "state_gla": n(ks[5], (DEPTH, DEC_BATCH, GLA_HEADS, GLA_DK, GLA_DV), f32),
        "w_in": n(ks[6], (DEPTH, D_MODEL, IN_WIDTH), f32) * D_MODEL ** -0.5,
        "w_a2": n(ks[7], (DEPTH, GLA_GATE_RANK, GLA_KEY_WIDTH), f32) * GLA_GATE_RANK ** -0.5,
        "b_a": 0.1 * n(ks[8], (DEPTH, GLA_KEY_WIDTH), f32),
        "b_f": 2.0 + 0.1 * n(ks[9], (DEPTH, FOX_HEADS), f32),
        "gla_gain": 1.0 + 0.02 * n(ks[10], (DEPTH, GLA_DV), f32),
        "w_oa": n(ks[11], (DEPTH, FOX_WIDTH, D_MODEL), f32) * FOX_WIDTH ** -0.5,
        "w_ob": n(ks[12], (DEPTH, GLA_VAL_WIDTH, D_MODEL), f32) * GLA_VAL_WIDTH ** -0.5,
        "w_out": n(ks[13], (DEPTH, D_MODEL, D_MODEL), f32) * D_MODEL ** -0.5,
        "pre_norm": 1.0 + 0.02 * n(ks[14], (DEPTH, D_MODEL), f32),
        "post_norm": 1.0 + 0.02 * n(ks[15], (DEPTH, D_MODEL), f32),
    }


def reference(x_prompt, x_sample, cache_k, cache_v, cache_logf, state_gla, w_in, w_a2, b_a, b_f, gla_gain, w_oa, w_ob, w_out, pre_norm, post_norm):
    yp, ys = x_prompt, x_sample
    bp = x_prompt.shape[0]
    kp, vp, fp, sp = [], [], [], []
    ksm, vsm, fsm, ssm = [], [], [], []
    for l in range(DEPTH):
        params = (w_in[l], w_a2[l], b_a[l], b_f[l], gla_gain[l], w_oa[l], w_ob[l], w_out[l], pre_norm[l], post_norm[l])
        empty_kv = jnp.zeros((bp, 0, FOX_HEADS, FOX_HEAD_DIM), x_prompt.dtype)
        empty_f = jnp.zeros((bp, 0, FOX_HEADS), jnp.float32)
        s0 = jnp.zeros((bp, GLA_HEADS, GLA_DK, GLA_DV), jnp.float32)
        yp, k1, v1, f1, s1 = mixer_layer(yp, empty_kv, empty_kv, empty_f, s0, *params)
        ys, k2, v2, f2, s2 = mixer_layer(ys, cache_k[l], cache_v[l], cache_logf[l], state_gla[l], *params)
        kp.append(k1); vp.append(v1); fp.append(f1); sp.append(s1)
        ksm.append(k2); vsm.append(v2); fsm.append(f2); ssm.append(s2)
    return (yp, ys, jnp.stack(kp), jnp.stack(vp), jnp.stack(fp), jnp.stack(sp), jnp.stack(ksm), jnp.stack(vsm), jnp.stack(fsm), jnp.stack(ssm))
```

```python
import functools

import jax
import jax.numpy as jnp
from jax import lax
from jax.experimental import pallas as pl
from jax.experimental.pallas import tpu as pltpu

GLA_CHUNK = 64
GLA_GATE_TAU = 16.0
NORM_EPS = 1e-6
GLA_SAFE_LOG_DECAY = -80.0

VMEM_LIMIT_BYTES = 56 * 1024 * 1024
F32 = jnp.float32
BF16 = jnp.bfloat16
HIGHEST = lax.Precision.HIGHEST


def _params(*semantics):
    return pltpu.CompilerParams(dimension_semantics=semantics, vmem_limit_bytes=VMEM_LIMIT_BYTES)


def _tile(n, pref):
    if n <= pref:
        return n
    t = pref
    while n % t:
        t //= 2
    return t


def _sigmoid(x):
    return 1.0 / (1.0 + jnp.exp(-x))


def _log_sigmoid(x):
    return jnp.minimum(x, 0.0) - jnp.log1p(jnp.exp(-jnp.abs(x)))


def _dot_nt(a, b):
    return lax.dot_general(a, b, (((1,), (1,)), ((), ())), preferred_element_type=F32)


def _dot_tn(a, b):
    return lax.dot_general(a, b, (((0,), (0,)), ((), ())), preferred_element_type=F32)


def _rmsnorm_kernel(x_ref, g_ref, o_ref):
    x = x_ref[...]
    ms = jnp.mean(x * x, axis=-1, keepdims=True)
    o_ref[...] = (x * lax.rsqrt(ms + NORM_EPS) * g_ref[...]).astype(o_ref.dtype)


def rmsnorm_bf16(x2d, gain):
    m, d = x2d.shape
    tm = _tile(m, 512)
    return pl.pallas_call(
        _rmsnorm_kernel,
        grid=(m // tm,),
        in_specs=[pl.BlockSpec((tm, d), lambda i: (i, 0)), pl.BlockSpec((1, d), lambda i: (0, 0))],
        out_specs=pl.BlockSpec((tm, d), lambda i: (i, 0)),
        out_shape=jax.ShapeDtypeStruct((m, d), BF16),
        compiler_params=_params("parallel"),
        name="rmsnorm_bf16",
    )(x2d, gain.reshape(1, d))


def _proj_kernel(h_ref, w_ref, *o_refs, kind, scale):
    acc = jnp.dot(h_ref[...], w_ref[...], preferred_element_type=F32)
    if kind == "scale":
        o_refs[0][...] = (acc * scale).astype(BF16)
    elif kind == "dual":
        o_refs[0][...] = acc
        o_refs[1][...] = acc.astype(BF16)
    elif kind == "plain":
        o_refs[0][...] = acc.astype(BF16)
    elif kind == "silu":
        o_refs[0][...] = (acc * _sigmoid(acc)).astype(BF16)
    elif kind == "sigmoid":
        o_refs[0][...] = _sigmoid(acc).astype(BF16)
    else:
        raise ValueError(kind)


def proj(h, w, kind, scale=1.0):
    m, k = h.shape
    n = w.shape[1]
    tm, tn = _tile(m, 512), _tile(n, 1024)
    out_dtypes = (F32, BF16) if kind == "dual" else (BF16,)
    out = pl.pallas_call(
        functools.partial(_proj_kernel, kind=kind, scale=scale),
        grid=(n // tn, m // tm),
        in_specs=[pl.BlockSpec((tm, k), lambda j, i: (i, 0)), pl.BlockSpec((k, tn), lambda j, i: (0, j))],
        out_specs=[pl.BlockSpec((tm, tn), lambda j, i: (i, j)) for _ in out_dtypes],
        out_shape=[jax.ShapeDtypeStruct((m, n), dt) for dt in out_dtypes],
        compiler_params=_params("parallel", "parallel"),
        name="proj_" + kind,
    )(h, w)
    return out if kind == "dual" else out[0]


def _proj_small_kernel(h_ref, w_ref, bf_ref, logf_ref, glr_ref, *, n_f):
    acc = jnp.dot(h_ref[...], w_ref[...], preferred_element_type=F32)
    logf_ref[...] = _log_sigmoid(acc[:, :n_f] + bf_ref[...])
    glr_ref[...] = acc[:, n_f:]


def proj_small(h, w_small, b_f):
    m, k = h.shape
    n_f = b_f.shape[0]
    n_r = w_small.shape[1] - n_f
    tm = _tile(m, 512)
    return pl.pallas_call(
        functools.partial(_proj_small_kernel, n_f=n_f),
        grid=(m // tm,),
        in_specs=[pl.BlockSpec((tm, k), lambda i: (i, 0)),
                  pl.BlockSpec((k, n_f + n_r), lambda i: (0, 0)),
                  pl.BlockSpec((1, n_f), lambda i: (0, 0))],
        out_specs=[pl.BlockSpec((tm, n_f), lambda i: (i, 0)), pl.BlockSpec((tm, n_r), lambda i: (i, 0))],
        out_shape=[jax.ShapeDtypeStruct((m, n_f), F32), jax.ShapeDtypeStruct((m, n_r), F32)],
        compiler_params=_params("parallel"),
        name="proj_small",
    )(h, w_small, b_f.reshape(1, n_f))


CUMSUM_LANES = 256


def _cumsum_kernel(x_ref, o_ref, *, length):
    rows = x_ref.shape[0]
    carry = jnp.zeros((rows, 1), F32)
    for start in range(0, length, CUMSUM_LANES):
        w = min(CUMSUM_LANES, length - start)
        r = lax.broadcasted_iota(jnp.int32, (w, w), 0)
        c = lax.broadcasted_iota(jnp.int32, (w, w), 1)
        upper = (r <= c).astype(F32)
        seg = jnp.dot(x_ref[:, start:start + w], upper, precision=HIGHEST, preferred_element_type=F32) + carry
        o_ref[:, start:start + w] = seg
        carry = seg[:, w - 1:w]


def cumsum_lanes(x):
    b, h, length = x.shape
    return pl.pallas_call(
        functools.partial(_cumsum_kernel, length=length),
        grid=(b,),
        in_specs=[pl.BlockSpec((None, h, length), lambda i: (i, 0, 0))],
        out_specs=pl.BlockSpec((None, h, length), lambda i: (i, 0, 0)),
        out_shape=jax.ShapeDtypeStruct((b, h, length), F32),
        compiler_params=_params("parallel"),
        name="cumsum_logf",
    )(x)


def _head_column(c_blk, h):
    lane = lax.broadcasted_iota(jnp.int32, c_blk.shape, 1)
    return jnp.sum(jnp.where(lane == h, c_blk, 0.0), axis=1, keepdims=True)


def _fox_prompt_kernel(q_ref, k_ref, v_ref, ct_ref, c_ref, g_ref, o_ref, m_sc, l_sc, acc_sc, *, tq, tk):
    h = pl.program_id(1)
    i = pl.program_id(2)
    q = q_ref[...]
    cq = _head_column(c_ref[...], h)
    m_sc[...] = jnp.full(m_sc.shape, -jnp.inf, F32)
    l_sc[...] = jnp.zeros(l_sc.shape, F32)
    acc_sc[...] = jnp.zeros(acc_sc.shape, F32)
    blocks_per_q = tq // tk

    def step(j, mask):
        start = pl.multiple_of(j * tk, tk)
        kj = k_ref[pl.ds(start, tk), :]
        vj = v_ref[pl.ds(start, tk), :]
        ck = ct_ref[pl.ds(h, 1), pl.ds(start, tk)]
        s = _dot_nt(q, kj) - ck
        if mask is not None:
            s = jnp.where(mask, s, -jnp.inf)
        m_prev = m_sc[...]
        m_new = jnp.maximum(m_prev, jnp.max(s, axis=1, keepdims=True) + cq)
        p = jnp.exp(s - (m_new - cq))
        alpha = jnp.exp(m_prev - m_new)
        l_sc[...] = alpha * l_sc[...] + jnp.sum(p, axis=1, keepdims=True)
        acc_sc[...] = alpha * acc_sc[...] + jnp.dot(p.astype(BF16), vj, preferred_element_type=F32)
        m_sc[...] = m_new

    def full_step(j, carry):
        step(j, None)
        return carry

    lax.fori_loop(0, i * blocks_per_q, full_step, 0)
    row = lax.broadcasted_iota(jnp.int32, (tq, tk), 0)
    col = lax.broadcasted_iota(jnp.int32, (tq, tk), 1)
    for d in range(blocks_per_q):
        step(i * blocks_per_q + d, col + d * tk <= row)
    o = acc_sc[...] / l_sc[...]
    o_ref[...] = (o * g_ref[...].astype(F32)).astype(o_ref.dtype)


def fox_prompt(q, k, v, c_t, c, gates, batch, length, heads, head_dim):
    tq = _tile(length, 512)
    tk = tq
    nq = length // tq
    return pl.pallas_call(
        functools.partial(_fox_prompt_kernel, tq=tq, tk=tk),
        grid=(batch, heads, nq),
        in_specs=[pl.BlockSpec((tq, head_dim), lambda b, h, i: (b * nq + i, h)),
                  pl.BlockSpec((length, head_dim), lambda b, h, i: (b, h)),
                  pl.BlockSpec((length, head_dim), lambda b, h, i: (b, h)),
                  pl.BlockSpec((None, heads, length), lambda b, h, i: (b, 0, 0)),
                  pl.BlockSpec((tq, heads), lambda b, h, i: (b * nq + i, 0)),
                  pl.BlockSpec((tq, head_dim), lambda b, h, i: (b * nq + i, h))],
        out_specs=pl.BlockSpec((tq, head_dim), lambda b, h, i: (b * nq + i, h)),
        out_shape=jax.ShapeDtypeStruct((batch * length, heads * head_dim), BF16),
        scratch_shapes=[pltpu.VMEM((tq, 1), F32), pltpu.VMEM((tq, 1), F32), pltpu.VMEM((tq, head_dim), F32)],
        compiler_params=_params("parallel", "parallel", "arbitrary"),
        name="fox_prompt",
    )(q, k, v, c_t, c, gates)


def _fox_cached_kernel(q_ref, kc_ref, vc_ref, kn_ref, vn_ref, ct_ref, c_ref, g_ref, o_ref, *, past, lq):
    h = pl.program_id(1)
    q = q_ref[...]
    cq = _head_column(c_ref[...], h)
    ct = ct_ref[...]
    sub = lax.broadcasted_iota(jnp.int32, ct.shape, 0)
    ck = jnp.sum(jnp.where(sub == h, ct, 0.0), axis=0, keepdims=True)
    ck_past = ck[:, :past]
    ck_new = ck[:, past:]
    s_past = _dot_nt(q, kc_ref[...].astype(BF16)) - ck_past
    s_new = _dot_nt(q, kn_ref[...]) - ck_new
    row = lax.broadcasted_iota(jnp.int32, (lq, lq), 0)
    col = lax.broadcasted_iota(jnp.int32, (lq, lq), 1)
    s_new = jnp.where(col <= row, s_new, -jnp.inf)
    m = jnp.maximum(jnp.max(s_past, axis=1, keepdims=True), jnp.max(s_new, axis=1, keepdims=True)) + cq
    p_past = jnp.exp(s_past - (m - cq))
    p_new = jnp.exp(s_new - (m - cq))
    l = jnp.sum(p_past, axis=1, keepdims=True) + jnp.sum(p_new, axis=1, keepdims=True)
    acc = (jnp.dot(p_past.astype(BF16), vc_ref[...].astype(BF16), preferred_element_type=F32)
           + jnp.dot(p_new.astype(BF16), vn_ref[...], preferred_element_type=F32))
    o_ref[...] = (acc / l * g_ref[...].astype(F32)).astype(o_ref.dtype)


def fox_cached(q, cache_k, cache_v, k_new, v_new, c_t, c, gates, batch, lq, heads, head_dim):
    past = cache_k.shape[1]
    return pl.pallas_call(
        functools.partial(_fox_cached_kernel, past=past, lq=lq),
        grid=(batch, heads),
        in_specs=[pl.BlockSpec((lq, head_dim), lambda b, h: (b, h)),
                  pl.BlockSpec((None, past, head_dim), lambda b, h: (b, 0, h)),
                  pl.BlockSpec((None, past, head_dim), lambda b, h: (b, 0, h)),
                  pl.BlockSpec((lq, head_dim), lambda b, h: (b, h)),
                  pl.BlockSpec((lq, head_dim), lambda b, h: (b, h)),
                  pl.BlockSpec((None, heads, past + lq), lambda b, h: (b, 0, 0)),
                  pl.BlockSpec((lq, heads), lambda b, h: (b, 0)),
                  pl.BlockSpec((lq, head_dim), lambda b, h: (b, h))],
        out_specs=pl.BlockSpec((lq, head_dim), lambda b, h: (b, h)),
        out_shape=jax.ShapeDtypeStruct((batch * lq, heads * head_dim), BF16),
        compiler_params=_params("parallel", "parallel"),
        name="fox_cached",
    )(q, cache_k, cache_v, k_new, v_new, c_t, c, gates)


def _gla_kernel(*refs, chunk, n_chunks, has_state):
    if has_state:
        (q_ref, k_ref, v_ref, glr_ref, wa_ref, ba_ref, gain_ref, gate_ref, s0_ref,
         o_ref, s_out_ref, st_sc, a_sc, q_sc, k_sc, b_sc) = refs
    else:
        (q_ref, k_ref, v_ref, glr_ref, wa_ref, ba_ref, gain_ref, gate_ref,
         o_ref, s_out_ref, st_sc, a_sc, q_sc, k_sc, b_sc) = refs
        s0_ref = None
    step = pl.program_id(2)

    @pl.when(step == 0)
    def _():
        if has_state:
            st_sc[...] = s0_ref[...].T
        else:
            st_sc[...] = jnp.zeros(st_sc.shape, F32)

    x = jnp.dot(glr_ref[...], wa_ref[...], precision=HIGHEST, preferred_element_type=F32) + ba_ref[...]
    log_a = _log_sigmoid(x) / GLA_GATE_TAU
    safe = jnp.min(log_a) * chunk >= GLA_SAFE_LOG_DECAY

    r = lax.broadcasted_iota(jnp.int32, (chunk, chunk), 0)
    c = lax.broadcasted_iota(jnp.int32, (chunk, chunk), 1)
    causal = c <= r
    tril = causal.astype(F32)

    for ci in range(n_chunks):
        rows = slice(ci * chunk, (ci + 1) * chunk)
        b = jnp.dot(tril, log_a[rows], precision=HIGHEST, preferred_element_type=F32)
        q = q_ref[rows, :].astype(F32)
        k = k_ref[rows, :].astype(F32)
        v = v_ref[rows, :]
        b_last = b[chunk - 1:chunk, :]
        q_dec = q * jnp.exp(b)

        @pl.when(safe)
        def _():
            k_inv = k * jnp.exp(-b)
            a_sc[...] = jnp.where(causal, _dot_nt(q_dec.astype(BF16), k_inv.astype(BF16)), 0.0)

        @pl.when(jnp.logical_not(safe))
        def _():
            q_sc[...] = q
            k_sc[...] = k
            b_sc[...] = b
            t_idx = lax.broadcasted_iota(jnp.int32, (chunk, 1), 0)
            s_lane = lax.broadcasted_iota(jnp.int32, (1, chunk), 1)

            def body(s, a):
                bs = b_sc[pl.ds(s, 1), :]
                ks = k_sc[pl.ds(s, 1), :]
                d = jnp.where(t_idx >= s, b_sc[...] - bs, -jnp.inf)
                col_s = jnp.sum(q_sc[...] * jnp.exp(d) * ks, axis=1, keepdims=True)
                return a + col_s * (s_lane == s).astype(F32)

            a_sc[...] = lax.fori_loop(0, chunk, body, jnp.zeros((chunk, chunk), F32))

        st = st_sc[...]
        o = (jnp.dot(a_sc[...].astype(BF16), v, preferred_element_type=F32)
             + _dot_nt(q_dec.astype(BF16), st.astype(BF16)))
        k_end = k * jnp.exp(b_last - b)
        st_sc[...] = st * jnp.exp(b_last) + _dot_tn(v, k_end.astype(BF16))
        o = o * lax.rsqrt(jnp.mean(o * o, axis=-1, keepdims=True) + NORM_EPS) * gain_ref[...]
        o_ref[rows, :] = (o * gate_ref[rows, :].astype(F32)).astype(o_ref.dtype)

    @pl.when(step == pl.num_programs(2) - 1)
    def _():
        s_out_ref[...] = st_sc[...].T


def gla(qkv, glr, w_a2, b_a, gain, gates, gate_col0, s0, batch, length, heads, dk, dv):
    assert dk == dv
    chunk = GLA_CHUNK if length % GLA_CHUNK == 0 else length
    rows = _tile(length, 512)
    rows = max(rows, chunk)
    n_steps = length // rows
    rank = glr.shape[1]
    has_state = s0 is not None
    gcol = gate_col0 // dv
    row_map = lambda off: (lambda b, h, i: (b * n_steps + i, off + h))
    in_specs = [pl.BlockSpec((rows, dk), row_map(0)),
                pl.BlockSpec((rows, dk), row_map(heads)),
                pl.BlockSpec((rows, dv), row_map(2 * heads)),
                pl.BlockSpec((rows, rank), lambda b, h, i: (b * n_steps + i, 0)),
                pl.BlockSpec((rank, dk), lambda b, h, i: (0, h)),
                pl.BlockSpec((1, dk), lambda b, h, i: (0, h)),
                pl.BlockSpec((1, dv), lambda b, h, i: (0, 0)),
                pl.BlockSpec((rows, dv), row_map(gcol))]
    args = [qkv, qkv, qkv, glr, w_a2, b_a.reshape(1, -1), gain.reshape(1, dv), gates]
    if has_state:
        in_specs.append(pl.BlockSpec((None, None, dk, dv), lambda b, h, i: (b, h, 0, 0)))
        args.append(s0)
    return pl.pallas_call(
        functools.partial(_gla_kernel, chunk=chunk, n_chunks=rows // chunk, has_state=has_state),
        grid=(batch, heads, n_steps),
        in_specs=in_specs,
        out_specs=[pl.BlockSpec((rows, dv), row_map(0)),
                   pl.BlockSpec((None, None, dk, dv), lambda b, h, i: (b, h, 0, 0))],
        out_shape=[jax.ShapeDtypeStruct((batch * length, heads * dv), BF16),
                   jax.ShapeDtypeStruct((batch, heads, dk, dv), F32)],
        scratch_shapes=[pltpu.VMEM((dv, dk), F32), pltpu.VMEM((chunk, chunk), F32),
                        pltpu.VMEM((chunk, dk), F32), pltpu.VMEM((chunk, dk), F32), pltpu.VMEM((chunk, dk), F32)],
        compiler_params=_params("parallel", "parallel", "arbitrary"),
        name="gla",
    )(*args)


def _out_kernel(oa_ref, ob_ref, sa_ref, sb_ref, x_ref, woa_ref, wob_ref, wout_ref, g_ref, y_ref):
    a = jnp.dot(oa_ref[...], woa_ref[...], preferred_element_type=F32)
    b = jnp.dot(ob_ref[...], wob_ref[...], preferred_element_type=F32)
    merged = sa_ref[...].astype(F32) * a + sb_ref[...].astype(F32) * b
    z = jnp.dot(merged.astype(BF16), wout_ref[...], preferred_element_type=F32)
    zn = z * lax.rsqrt(jnp.mean(z * z, axis=-1, keepdims=True) + NORM_EPS) * g_ref[...]
    y_ref[...] = x_ref[...] + zn


def out_proj(o_a, o_b, merge_gates, x2d, w_oa, w_ob, w_out, post_gain):
    m, d = x2d.shape
    wa, wb = o_a.shape[1], o_b.shape[1]
    tm = _tile(m, 256)
    resident = lambda shape: pl.BlockSpec(shape, lambda i: (0, 0), pipeline_mode=pl.Buffered(1))
    return pl.pallas_call(
        _out_kernel,
        grid=(m // tm,),
        in_specs=[pl.BlockSpec((tm, wa), lambda i: (i, 0)),
                  pl.BlockSpec((tm, wb), lambda i: (i, 0)),
                  pl.BlockSpec((tm, d), lambda i: (i, 0)),
                  pl.BlockSpec((tm, d), lambda i: (i, 1)),
                  pl.BlockSpec((tm, d), lambda i: (i, 0)),
                  resident((wa, d)), resident((wb, d)), resident((d, d)), resident((1, d))],
        out_specs=pl.BlockSpec((tm, d), lambda i: (i, 0)),
        out_shape=jax.ShapeDtypeStruct((m, d), F32),
        compiler_params=_params("parallel"),
        name="out_proj",
    )(o_a, o_b, merge_gates, merge_gates, x2d, w_oa, w_ob, w_out, post_gain.reshape(1, d))


def _split_weights(w_in, sizes):
    parts, off = [], 0
    for s in sizes:
        parts.append(w_in[:, off:off + s])
        off += s
    return parts


def _layer_weights(w_in, w_oa, w_ob, w_out, fox_width, fox_heads, gla_width, rank, d_model, dk):
    sizes = (fox_width, fox_width, fox_width, fox_heads, fox_width,
             gla_width, gla_width, gla_width, rank, gla_width, d_model, d_model)
    fq, fk, fv, ff, fg, gq, gk, gv, glr, gg, ma, mb = _split_weights(w_in, sizes)
    bf = lambda *ws: jnp.concatenate([w.astype(BF16) for w in ws], axis=1) if len(ws) > 1 else ws[0].astype(BF16)
    return dict(
        w_q=bf(fq), w_k=bf(fk), w_v=bf(fv),
        w_gla=bf(gq * (dk ** -0.5), gk, gv),
        w_silu=bf(fg, gg), w_merge=bf(ma, mb), w_small=bf(ff, glr),
        w_oa=w_oa.astype(BF16), w_ob=w_ob.astype(BF16), w_out=w_out.astype(BF16))


def _mixer_group(x2d, batch, length, wts, w_a2, b_a, b_f, gla_gain, pre_g, post_g, dims, past):
    fox_heads, head_dim, gla_heads, dk, dv = dims
    fox_width = fox_heads * head_dim
    h = rmsnorm_bf16(x2d, pre_g)
    q = proj(h, wts["w_q"], "scale", head_dim ** -0.5)
    k32, k16 = proj(h, wts["w_k"], "dual")
    v32, v16 = proj(h, wts["w_v"], "dual")
    qkv_gla = proj(h, wts["w_gla"], "plain")
    gates = proj(h, wts["w_silu"], "silu")
    merge_gates = proj(h, wts["w_merge"], "sigmoid")
    logf, glr = proj_small(h, wts["w_small"], b_f)
    logf3 = logf.reshape(batch, length, fox_heads)

    if past is None:
        c_t = cumsum_lanes(jnp.transpose(logf3, (0, 2, 1)))
        c = jnp.transpose(c_t, (0, 2, 1)).reshape(batch * length, fox_heads)
        o_a = fox_prompt(q, k16, v16, c_t, c, gates, batch, length, fox_heads, head_dim)
        s0 = None
    else:
        cache_k, cache_v, cache_logf, s0 = past
        n_past = cache_k.shape[1]
        logf_all = jnp.concatenate([cache_logf.astype(F32), logf3], axis=1)
        c_t = cumsum_lanes(jnp.transpose(logf_all, (0, 2, 1)))
        c = jnp.transpose(c_t[:, :, n_past:], (0, 2, 1)).reshape(batch * length, fox_heads)
        o_a = fox_cached(q, cache_k.reshape(batch, n_past, fox_width), cache_v.reshape(batch, n_past, fox_width),
                         k16, v16, c_t, c, gates, batch, length, fox_heads, head_dim)

    o_b, s_new = gla(qkv_gla, glr, w_a2, b_a, gla_gain, gates, fox_width, s0, batch, length, gla_heads, dk, dv)
    y = out_proj(o_a, o_b, merge_gates, x2d, wts["w_oa"], wts["w_ob"], wts["w_out"], post_g)
    kv_shape = (batch, length, fox_heads, head_dim)
    return y, k32.reshape(kv_shape), v32.reshape(kv_shape), logf3, s_new


def kernel(x_prompt, x_sample, cache_k, cache_v, cache_logf, state_gla, w_in, w_a2, b_a, b_f, gla_gain, w_oa, w_ob, w_out, pre_norm, post_norm):
    depth = w_in.shape[0]
    bp, lp, d_model = x_prompt.shape
    bs, ls, _ = x_sample.shape
    fox_heads, head_dim = cache_k.shape[3], cache_k.shape[4]
    gla_heads, dk, dv = state_gla.shape[2], state_gla.shape[3], state_gla.shape[4]
    rank = w_a2.shape[1]
    dims = (fox_heads, head_dim, gla_heads, dk, dv)
    yp = x_prompt.reshape(bp * lp, d_model)
    ys = x_sample.reshape(bs * ls, d_model)
    outs_p, outs_s = [], []
    for l in range(depth):
        wts = _layer_weights(w_in[l], w_oa[l], w_ob[l], w_out[l], fox_heads * head_dim, fox_heads,
                             gla_heads * dk, rank, d_model, dk)
        shared = (wts, w_a2[l], b_a[l], b_f[l], gla_gain[l], pre_norm[l], post_norm[l], dims)
        yp, *rest_p = _mixer_group(yp, bp, lp, *shared, past=None)
        ys, *rest_s = _mixer_group(ys, bs, ls, *shared, past=(cache_k[l], cache_v[l], cache_logf[l], state_gla[l]))
        outs_p.append(rest_p)
        outs_s.append(rest_s)
    stack = lambda outs, i: jnp.stack([o[i] for o in outs])
    return (yp.reshape(bp, lp, d_model), ys.reshape(bs, ls, d_model),
            stack(outs_p, 0), stack(outs_p, 1), stack(outs_p, 2), stack(outs_p, 3),
            stack(outs_s, 0), stack(outs_s, 1), stack(outs_s, 2), stack(outs_s, 3))
```

```python
import functools

import jax
import jax.numpy as jnp
from jax import lax
from jax.experimental import pallas as pl
from jax.experimental.pallas import tpu as pltpu

GLA_CHUNK = 64
GLA_GATE_TAU = 16.0
NORM_EPS = 1e-6
GLA_SAFE_LOG_DECAY = -80.0

VMEM_LIMIT_BYTES = 56 * 1024 * 1024
F32 = jnp.float32
BF16 = jnp.bfloat16


def _params(*semantics):
    return pltpu.CompilerParams(dimension_semantics=semantics, vmem_limit_bytes=VMEM_LIMIT_BYTES)


def _tile(n, pref):
    if n <= pref:
        return n
    t = pref
    while n % t:
        t //= 2
    return t


def _sigmoid(x):
    return 1.0 / (1.0 + jnp.exp(-x))


def _log_sigmoid(x):
    return jnp.minimum(x, 0.0) - jnp.log1p(jnp.exp(-jnp.abs(x)))


def _dot_nt(a, b):
    return lax.dot_general(a, b, (((1,), (1,)), ((), ())), preferred_element_type=F32)


def _dot_tn(a, b):
    return lax.dot_general(a, b, (((0,), (0,)), ((), ())), preferred_element_type=F32)


def _split3(x):
    hi = x.astype(BF16)
    r1 = x - hi.astype(F32)
    mid = r1.astype(BF16)
    lo = (r1 - mid.astype(F32)).astype(BF16)
    return hi, mid, lo


def _rmsnorm_kernel(x_ref, g_ref, o_ref):
    x = x_ref[...]
    ms = jnp.mean(x * x, axis=-1, keepdims=True)
    o_ref[...] = (x * lax.rsqrt(ms + NORM_EPS) * g_ref[...]).astype(o_ref.dtype)


def rmsnorm_bf16(x2d, gain):
    m, d = x2d.shape
    tm = _tile(m, 512)
    return pl.pallas_call(
        _rmsnorm_kernel,
        grid=(m // tm,),
        in_specs=[pl.BlockSpec((tm, d), lambda i: (i, 0)), pl.BlockSpec((1, d), lambda i: (0, 0))],
        out_specs=pl.BlockSpec((tm, d), lambda i: (i, 0)),
        out_shape=jax.ShapeDtypeStruct((m, d), BF16),
        compiler_params=_params("parallel"),
        name="rmsnorm_bf16",
    )(x2d, gain.reshape(1, d))


LANES = 128
PROJ_TN = 1024
PROJ_TM = 512
CAST_ROWS = 256


def _epilogue(acc, o_refs, kind, scale, col_tile, first_tiles):
    if kind == "scale":
        o_refs[0][...] = (acc * scale).astype(BF16)
    elif kind == "scale_first":
        o_refs[0][...] = (acc * jnp.where(col_tile < first_tiles, scale, 1.0)).astype(BF16)
    elif kind == "dual":
        o_refs[0][...] = acc
        o_refs[1][...] = acc.astype(BF16)
    elif kind == "silu":
        o_refs[0][...] = (acc * _sigmoid(acc)).astype(BF16)
    elif kind == "sigmoid":
        o_refs[0][...] = _sigmoid(acc).astype(BF16)
    else:
        raise ValueError(kind)


def _proj_kernel(*refs, kind, scale, shift, n_w, uses_tail, n_out, mp_tiles, first_tiles):
    hp_ref, hs_ref = refs[:2]
    w_refs = refs[2:2 + n_w]
    pos = 2 + n_w
    tail_ref = refs[pos] if uses_tail else None
    pos += int(uses_tail)
    outs_p = refs[pos:pos + n_out]
    outs_s = refs[pos + n_out:pos + 2 * n_out]
    w_sc = refs[pos + 2 * n_out]
    n = pl.program_id(0)
    m = pl.program_id(1)

    @pl.when(m == 0)
    def _():
        tn = w_sc.shape[1]
        for r0 in range(0, w_sc.shape[0], CAST_ROWS):
            tiles = [r[r0:r0 + CAST_ROWS, :] for r in w_refs]
            if uses_tail:
                tiles[-1] = jnp.where(n == pl.num_programs(0) - 1, tail_ref[r0:r0 + CAST_ROWS, :], tiles[-1])
            w = jnp.concatenate(tiles, axis=1)
            w_sc[r0:r0 + CAST_ROWS, :] = w[:, shift:shift + tn].astype(BF16)

    @pl.when(m < mp_tiles)
    def _():
        _epilogue(jnp.dot(hp_ref[...], w_sc[...], preferred_element_type=F32), outs_p, kind, scale, n, first_tiles)

    @pl.when(m == mp_tiles)
    def _():
        _epilogue(jnp.dot(hs_ref[...], w_sc[...], preferred_element_type=F32), outs_s, kind, scale, n, first_tiles)


def proj(hp, hs, w_in, w_tail, layer, col0, n_cols, seg_cols, kind, scale=1.0):
    mp, k = hp.shape
    ms = hs.shape[0]
    tn = _tile(seg_cols, PROJ_TN)
    assert n_cols % seg_cols == 0 and mp % PROJ_TM == 0 and tn % LANES == 0
    n_tiles, mp_tiles = n_cols // tn, mp // PROJ_TM
    shift, base0 = col0 % LANES, col0 // LANES
    w_per_tile = tn // LANES
    n_w = w_per_tile + (1 if shift else 0)
    full_tiles = w_in.shape[2] // LANES
    last_needed = base0 + w_per_tile * (n_tiles - 1) + n_w - 1
    assert last_needed <= full_tiles
    uses_tail = last_needed == full_tiles
    out_dtypes = (F32, BF16) if kind == "dual" else (BF16,)
    n_out = len(out_dtypes)

    def w_spec(t):
        return pl.BlockSpec((None, k, LANES),
                            lambda n, m: (layer, 0, jnp.minimum(base0 + w_per_tile * n + t, full_tiles - 1)))

    in_specs = ([pl.BlockSpec((PROJ_TM, k), lambda n, m: (jnp.minimum(m, mp_tiles - 1), 0)),
                 pl.BlockSpec((ms, k), lambda n, m: (0, 0))]
                + [w_spec(t) for t in range(n_w)])
    args = [hp, hs] + [w_in] * n_w
    if uses_tail:
        in_specs.append(pl.BlockSpec((None, k, LANES), lambda n, m: (layer, 0, 0)))
        args.append(w_tail)
    out = pl.pallas_call(
        functools.partial(_proj_kernel, kind=kind, scale=scale, shift=shift, n_w=n_w, uses_tail=uses_tail,
                          n_out=n_out, mp_tiles=mp_tiles, first_tiles=seg_cols // tn),
        grid=(n_tiles, mp_tiles + 1),
        in_specs=in_specs,
        out_specs=([pl.BlockSpec((PROJ_TM, tn), lambda n, m: (jnp.minimum(m, mp_tiles - 1), n)) for _ in out_dtypes]
                   + [pl.BlockSpec((ms, tn), lambda n, m: (0, n)) for _ in out_dtypes]),
        out_shape=([jax.ShapeDtypeStruct((mp, n_cols), dt) for dt in out_dtypes]
                   + [jax.ShapeDtypeStruct((ms, n_cols), dt) for dt in out_dtypes]),
        scratch_shapes=[pltpu.VMEM((k, tn), BF16)],
        compiler_params=_params("arbitrary", "arbitrary"),
        name="proj_" + kind,
    )(*args)
    return tuple(out[:n_out]), tuple(out[n_out:])


def _proj_small_kernel(h_ref, w_ref, bf_ref, logf_ref, glr_ref, *, n_f):
    acc = jnp.dot(h_ref[...], w_ref[...], preferred_element_type=F32)
    logf_ref[...] = _log_sigmoid(acc[:, :n_f] + bf_ref[...])
    glr_ref[...] = acc[:, n_f:]


def proj_small(h, w_small, b_f):
    m, k = h.shape
    n_f = b_f.shape[0]
    n_r = w_small.shape[1] - n_f
    tm = _tile(m, 512)
    return pl.pallas_call(
        functools.partial(_proj_small_kernel, n_f=n_f),
        grid=(m // tm,),
        in_specs=[pl.BlockSpec((tm, k), lambda i: (i, 0)),
                  pl.BlockSpec((k, n_f + n_r), lambda i: (0, 0)),
                  pl.BlockSpec((1, n_f), lambda i: (0, 0))],
        out_specs=[pl.BlockSpec((tm, n_f), lambda i: (i, 0)), pl.BlockSpec((tm, n_r), lambda i: (i, 0))],
        out_shape=[jax.ShapeDtypeStruct((m, n_f), F32), jax.ShapeDtypeStruct((m, n_r), F32)],
        compiler_params=_params("parallel"),
        name="proj_small",
    )(h, w_small, b_f.reshape(1, n_f))


CUMSUM_LANES = 256


def _cumsum_kernel(x_ref, o_ref, *, length):
    rows = x_ref.shape[0]
    carry = jnp.zeros((rows, 1), F32)
    for start in range(0, length, CUMSUM_LANES):
        w = min(CUMSUM_LANES, length - start)
        r = lax.broadcasted_iota(jnp.int32, (w, w), 0)
        c = lax.broadcasted_iota(jnp.int32, (w, w), 1)
        upper = (r <= c).astype(BF16)
        hi, mid, lo = _split3(x_ref[:, start:start + w])
        dot = lambda a: jnp.dot(a, upper, preferred_element_type=F32)
        seg = (dot(lo) + dot(mid)) + dot(hi) + carry
        o_ref[:, start:start + w] = seg
        carry = seg[:, w - 1:w]


def cumsum_lanes(x):
    b, h, length = x.shape
    return pl.pallas_call(
        functools.partial(_cumsum_kernel, length=length),
        grid=(b,),
        in_specs=[pl.BlockSpec((None, h, length), lambda i: (i, 0, 0))],
        out_specs=pl.BlockSpec((None, h, length), lambda i: (i, 0, 0)),
        out_shape=jax.ShapeDtypeStruct((b, h, length), F32),
        compiler_params=_params("parallel"),
        name="cumsum_logf",
    )(x)


LOG2E = 1.4426950408889634
N_SPLIT = 3


def _kaug_kernel(k_ref, c_ref, o_ref, *, heads, dh):
    hi, mid, lo = (piece.astype(F32) for piece in _split3(c_ref[...] * LOG2E))
    lane = lax.broadcasted_iota(jnp.int32, (hi.shape[0], dh), 1)
    for h in range(heads):
        extra = jnp.where(lane == 0, hi[:, h:h + 1],
                          jnp.where(lane == 1, mid[:, h:h + 1], jnp.where(lane == 2, lo[:, h:h + 1], 0.0)))
        o_ref[:, 2 * h * dh:(2 * h + 1) * dh] = k_ref[:, h * dh:(h + 1) * dh]
        o_ref[:, (2 * h + 1) * dh:(2 * h + 2) * dh] = extra.astype(BF16)


def key_augment(k16, c, heads, dh):
    m = k16.shape[0]
    tm = _tile(m, 512)
    return pl.pallas_call(
        functools.partial(_kaug_kernel, heads=heads, dh=dh),
        grid=(m // tm,),
        in_specs=[pl.BlockSpec((tm, heads * dh), lambda i: (i, 0)), pl.BlockSpec((tm, heads), lambda i: (i, 0))],
        out_specs=pl.BlockSpec((tm, 2 * heads * dh), lambda i: (i, 0)),
        out_shape=jax.ShapeDtypeStruct((m, 2 * heads * dh), BF16),
        compiler_params=_params("parallel"),
        name="key_augment",
    )(k16, c)


def _fox_prompt_kernel(q_ref, ka_ref, v_ref, ct_ref, g_ref, o_ref, m_sc, l_sc, acc_sc, sa_sc, sb_sc, mxa_sc, mxb_sc,
                       *, t, n_tiles):
    h = pl.program_id(1)
    i = pl.program_id(2)
    dh = q_ref.shape[1]
    lane = lax.broadcasted_iota(jnp.int32, (t, dh), 1)
    q_aug = jnp.concatenate([q_ref[...], jnp.where(lane < N_SPLIT, -1.0, 0.0).astype(BF16)], axis=1)
    cq = ct_ref[pl.ds(h, 1), pl.ds(pl.multiple_of(i * t, t), t)] * LOG2E
    m_sc[...] = jnp.full(m_sc.shape, -jnp.inf, F32)
    l_sc[...] = jnp.zeros(l_sc.shape, F32)
    acc_sc[...] = jnp.zeros(acc_sc.shape, F32)
    key = lax.broadcasted_iota(jnp.int32, (t, t), 0)
    qry = lax.broadcasted_iota(jnp.int32, (t, t), 1)

    def rows(j):
        return pl.ds(pl.multiple_of(jnp.minimum(j, n_tiles - 1) * t, t), t)

    def scores(j, s_sc, mx_sc):
        s = _dot_nt(ka_ref[rows(j), :], q_aug)
        s_sc[...] = s
        mx_sc[...] = jnp.max(s, axis=0, keepdims=True)

    def accumulate(j, s_sc, mx_sc, masked):
        s = s_sc[...]
        if masked:
            s = jnp.where(key + (j - i) * t <= qry, s, -jnp.inf)
            m_cur = jnp.max(s, axis=0, keepdims=True)
        else:
            m_cur = mx_sc[...]
        m_prev = m_sc[...]
        m_new = jnp.maximum(m_prev, m_cur + cq)
        p = jnp.exp2(s - (m_new - cq))
        alpha = jnp.exp2(m_prev - m_new)
        l_sc[...] = alpha * l_sc[...] + jnp.sum(p, axis=0, keepdims=True)
        acc_sc[...] = alpha * acc_sc[...] + _dot_tn(v_ref[rows(j), :], p.astype(BF16))
        m_sc[...] = m_new

    def pair(p, carry):
        scores(2 * p + 1, sb_sc, mxb_sc)
        accumulate(2 * p, sa_sc, mxa_sc, False)
        scores(2 * p + 2, sa_sc, mxa_sc)
        accumulate(2 * p + 1, sb_sc, mxb_sc, False)
        return carry

    t0 = (i // 2) * 2
    scores(0, sa_sc, mxa_sc)
    lax.fori_loop(0, i // 2, pair, 0)
    scores(t0 + 1, sb_sc, mxb_sc)
    accumulate(t0, sa_sc, mxa_sc, True)
    accumulate(t0 + 1, sb_sc, mxb_sc, True)
    o = (acc_sc[...] / l_sc[...]).T
    o_ref[...] = (o * g_ref[...].astype(F32)).astype(o_ref.dtype)


def fox_prompt(q, k_aug, v, c_t, gates, batch, length, heads, head_dim):
    tq = _tile(length, 512)
    nq = length // tq
    return pl.pallas_call(
        functools.partial(_fox_prompt_kernel, t=tq, n_tiles=nq),
        grid=(batch, heads, nq),
        in_specs=[pl.BlockSpec((tq, head_dim), lambda b, h, i: (b * nq + i, h)),
                  pl.BlockSpec((length, 2 * head_dim), lambda b, h, i: (b, h)),
                  pl.BlockSpec((length, head_dim), lambda b, h, i: (b, h)),
                  pl.BlockSpec((None, heads, length), lambda b, h, i: (b, 0, 0)),
                  pl.BlockSpec((tq, head_dim), lambda b, h, i: (b * nq + i, h))],
        out_specs=pl.BlockSpec((tq, head_dim), lambda b, h, i: (b * nq + i, h)),
        out_shape=jax.ShapeDtypeStruct((batch * length, heads * head_dim), BF16),
        scratch_shapes=[pltpu.VMEM((1, tq), F32), pltpu.VMEM((1, tq), F32), pltpu.VMEM((head_dim, tq), F32),
                        pltpu.VMEM((tq, tq), F32), pltpu.VMEM((tq, tq), F32),
                        pltpu.VMEM((1, tq), F32), pltpu.VMEM((1, tq), F32)],
        compiler_params=_params("parallel", "parallel", "arbitrary"),
        name="fox_prompt",
    )(q, k_aug, v, c_t, gates)


def _fox_cached_kernel(q_ref, kc_ref, vc_ref, kn_ref, vn_ref, ct_ref, c_ref, g_ref, o_ref, m_sc, l_sc, acc_sc,
                       *, heads, dh, lq, tk, past):
    j = pl.program_id(1)

    @pl.when(j == 0)
    def _():
        m_sc[...] = jnp.full(m_sc.shape, -jnp.inf, F32)
        l_sc[...] = jnp.zeros(l_sc.shape, F32)
        acc_sc[...] = jnp.zeros(acc_sc.shape, F32)

    cq_all = c_ref[...] * LOG2E

    def update(h, s, v):
        rows = slice(h * lq, (h + 1) * lq)
        cq = cq_all[:, h:h + 1]
        m_prev = m_sc[rows, :]
        m_new = jnp.maximum(m_prev, jnp.max(s, axis=1, keepdims=True) + cq)
        p = jnp.exp2(s - (m_new - cq))
        alpha = jnp.exp2(m_prev - m_new)
        l_sc[rows, :] = alpha * l_sc[rows, :] + jnp.sum(p, axis=1, keepdims=True)
        acc_sc[rows, :] = alpha * acc_sc[rows, :] + jnp.dot(p.astype(BF16), v, preferred_element_type=F32)
        m_sc[rows, :] = m_new

    ck_tile = ct_ref[:, pl.ds(pl.multiple_of(j * tk, tk), tk)] * LOG2E
    for h in range(heads):
        kh = kc_ref[pl.ds(h, tk, stride=heads), :].astype(BF16)
        vh = vc_ref[pl.ds(h, tk, stride=heads), :].astype(BF16)
        s = _dot_nt(q_ref[:, h * dh:(h + 1) * dh], kh) - ck_tile[h:h + 1, :]
        update(h, s, vh)

    @pl.when(j == pl.num_programs(1) - 1)
    def _():
        ck_new = ct_ref[:, past:] * LOG2E
        row = lax.broadcasted_iota(jnp.int32, (lq, lq), 0)
        col = lax.broadcasted_iota(jnp.int32, (lq, lq), 1)
        for h in range(heads):
            cols = slice(h * dh, (h + 1) * dh)
            s = _dot_nt(q_ref[:, cols], kn_ref[:, cols]) - ck_new[h:h + 1, :]
            update(h, jnp.where(col <= row, s, -jnp.inf), vn_ref[:, cols])
            rows = slice(h * lq, (h + 1) * lq)
            o = acc_sc[rows, :] / l_sc[rows, :]
            o_ref[:, cols] = (o * g_ref[:, cols].astype(F32)).astype(o_ref.dtype)


def fox_cached(q, cache_k, cache_v, layer, k_new, v_new, c_t, c, gates, batch, lq, heads, head_dim):
    past = cache_k.shape[2] // heads
    tk = _tile(past, 512)
    width = heads * head_dim
    cache_spec = pl.BlockSpec((None, None, tk * heads, head_dim), lambda b, j: (layer, b, j, 0))
    row_spec = pl.BlockSpec((lq, width), lambda b, j: (b, 0))
    return pl.pallas_call(
        functools.partial(_fox_cached_kernel, heads=heads, dh=head_dim, lq=lq, tk=tk, past=past),
        grid=(batch, past // tk),
        in_specs=[row_spec, cache_spec, cache_spec, row_spec, row_spec,
                  pl.BlockSpec((None, heads, past + lq), lambda b, j: (b, 0, 0)),
                  pl.BlockSpec((lq, heads), lambda b, j: (b, 0)),
                  row_spec],
        out_specs=row_spec,
        out_shape=jax.ShapeDtypeStruct((batch * lq, width), BF16),
        scratch_shapes=[pltpu.VMEM((heads * lq, 1), F32), pltpu.VMEM((heads * lq, 1), F32),
                        pltpu.VMEM((heads * lq, head_dim), F32)],
        compiler_params=_params("parallel", "arbitrary"),
        name="fox_cached",
    )(q, cache_k, cache_v, k_new, v_new, c_t, c, gates)


def _gla_kernel(*refs, chunk, n_chunks, has_state):
    if has_state:
        (q_ref, k_ref, v_ref, glr_ref, wa_ref, ba_ref, gain_ref, gate_ref, s0_ref,
         o_ref, s_out_ref, st_sc, a_sc, q_sc, k_sc, b_sc) = refs
    else:
        (q_ref, k_ref, v_ref, glr_ref, wa_ref, ba_ref, gain_ref, gate_ref,
         o_ref, s_out_ref, st_sc, a_sc, q_sc, k_sc, b_sc) = refs
        s0_ref = None
    step = pl.program_id(2)

    @pl.when(step == 0)
    def _():
        if has_state:
            st_sc[...] = s0_ref[...].T
        else:
            st_sc[...] = jnp.zeros(st_sc.shape, F32)

    x = jnp.dot(glr_ref[...].astype(BF16), wa_ref[...].astype(BF16), preferred_element_type=F32) + ba_ref[...]
    log_a = _log_sigmoid(x) / GLA_GATE_TAU
    safe = jnp.min(log_a) * chunk >= GLA_SAFE_LOG_DECAY

    r = lax.broadcasted_iota(jnp.int32, (chunk, chunk), 0)
    c = lax.broadcasted_iota(jnp.int32, (chunk, chunk), 1)
    causal = c <= r
    tril = causal.astype(BF16)

    for ci in range(n_chunks):
        rows = slice(ci * chunk, (ci + 1) * chunk)
        g_hi, g_mid, g_lo = _split3(log_a[rows])
        cum = lambda a: jnp.dot(tril, a, preferred_element_type=F32)
        b = (cum(g_lo) + cum(g_mid)) + cum(g_hi)
        q = q_ref[rows, :].astype(F32)
        k = k_ref[rows, :].astype(F32)
        v = v_ref[rows, :]
        b_last = b[chunk - 1:chunk, :]
        q_dec = q * jnp.exp(b)

        @pl.when(safe)
        def _():
            k_inv = k * jnp.exp(-b)
            a_sc[...] = jnp.where(causal, _dot_nt(q_dec.astype(BF16), k_inv.astype(BF16)), 0.0)

        @pl.when(jnp.logical_not(safe))
        def _():
            q_sc[...] = q
            k_sc[...] = k
            b_sc[...] = b
            t_idx = lax.broadcasted_iota(jnp.int32, (chunk, 1), 0)
            s_lane = lax.broadcasted_iota(jnp.int32, (1, chunk), 1)

            def body(s, a):
                bs = b_sc[pl.ds(s, 1), :]
                ks = k_sc[pl.ds(s, 1), :]
                d = jnp.where(t_idx >= s, b_sc[...] - bs, -jnp.inf)
                col_s = jnp.sum(q_sc[...] * jnp.exp(d) * ks, axis=1, keepdims=True)
                return a + col_s * (s_lane == s).astype(F32)

            a_sc[...] = lax.fori_loop(0, chunk, body, jnp.zeros((chunk, chunk), F32))

        st = st_sc[...]
        o = (jnp.dot(a_sc[...].astype(BF16), v, preferred_element_type=F32)
             + _dot_nt(q_dec.astype(BF16), st.astype(BF16)))
        k_end = k * jnp.exp(b_last - b)
        st_sc[...] = st * jnp.exp(b_last) + _dot_tn(v, k_end.astype(BF16))
        o = o * lax.rsqrt(jnp.mean(o * o, axis=-1, keepdims=True) + NORM_EPS) * gain_ref[...]
        o_ref[rows, :] = (o * gate_ref[rows, :].astype(F32)).astype(o_ref.dtype)

    @pl.when(step == pl.num_programs(2) - 1)
    def _():
        s_out_ref[...] = st_sc[...].T


def gla(qkv, glr, w_a2, b_a, gain, gates, s0, batch, length, heads, dk, dv):
    assert dk == dv
    chunk = GLA_CHUNK if length % GLA_CHUNK == 0 else length
    rows = max(_tile(length, 512), chunk)
    n_steps = length // rows
    rank = glr.shape[1]
    has_state = s0 is not None
    row_map = lambda off: (lambda b, h, i: (b * n_steps + i, off + h))
    in_specs = [pl.BlockSpec((rows, dk), row_map(0)),
                pl.BlockSpec((rows, dk), row_map(heads)),
                pl.BlockSpec((rows, dv), row_map(2 * heads)),
                pl.BlockSpec((rows, rank), lambda b, h, i: (b * n_steps + i, 0)),
                pl.BlockSpec((rank, dk), lambda b, h, i: (0, h)),
                pl.BlockSpec((1, dk), lambda b, h, i: (0, h)),
                pl.BlockSpec((1, dv), lambda b, h, i: (0, 0)),
                pl.BlockSpec((rows, dv), row_map(0))]
    args = [qkv, qkv, qkv, glr, w_a2, b_a.reshape(1, -1), gain.reshape(1, dv), gates]
    if has_state:
        in_specs.append(pl.BlockSpec((None, None, dk, dv), lambda b, h, i: (b, h, 0, 0)))
        args.append(s0)
    return pl.pallas_call(
        functools.partial(_gla_kernel, chunk=chunk, n_chunks=rows // chunk, has_state=has_state),
        grid=(batch, heads, n_steps),
        in_specs=in_specs,
        out_specs=[pl.BlockSpec((rows, dv), row_map(0)),
                   pl.BlockSpec((None, None, dk, dv), lambda b, h, i: (b, h, 0, 0))],
        out_shape=[jax.ShapeDtypeStruct((batch * length, heads * dv), BF16),
                   jax.ShapeDtypeStruct((batch, heads, dk, dv), F32)],
        scratch_shapes=[pltpu.VMEM((dv, dk), F32), pltpu.VMEM((chunk, chunk), F32),
                        pltpu.VMEM((chunk, dk), F32), pltpu.VMEM((chunk, dk), F32), pltpu.VMEM((chunk, dk), F32)],
        compiler_params=_params("parallel", "parallel", "arbitrary"),
        name="gla",
    )(*args)


def _out_kernel(oa_ref, ob_ref, sa_ref, sb_ref, x_ref, woa_ref, wob_ref, wout_ref, g_ref, y_ref):
    a = jnp.dot(oa_ref[...], woa_ref[...], preferred_element_type=F32)
    b = jnp.dot(ob_ref[...], wob_ref[...], preferred_element_type=F32)
    merged = sa_ref[...].astype(F32) * a + sb_ref[...].astype(F32) * b
    z = jnp.dot(merged.astype(BF16), wout_ref[...], preferred_element_type=F32)
    zn = z * lax.rsqrt(jnp.mean(z * z, axis=-1, keepdims=True) + NORM_EPS) * g_ref[...]
    y_ref[...] = x_ref[...] + zn


def out_proj(o_a, o_b, merge_gates, x2d, w_oa, w_ob, w_out, post_gain):
    m, d = x2d.shape
    wa, wb = o_a.shape[1], o_b.shape[1]
    tm = _tile(m, 256)
    resident = lambda shape: pl.BlockSpec(shape, lambda i: (0, 0), pipeline_mode=pl.Buffered(1))
    return pl.pallas_call(
        _out_kernel,
        grid=(m // tm,),
        in_specs=[pl.BlockSpec((tm, wa), lambda i: (i, 0)),
                  pl.BlockSpec((tm, wb), lambda i: (i, 0)),
                  pl.BlockSpec((tm, d), lambda i: (i, 0)),
                  pl.BlockSpec((tm, d), lambda i: (i, 1)),
                  pl.BlockSpec((tm, d), lambda i: (i, 0)),
                  resident((wa, d)), resident((wb, d)), resident((d, d)), resident((1, d))],
        out_specs=pl.BlockSpec((tm, d), lambda i: (i, 0)),
        out_shape=jax.ShapeDtypeStruct((m, d), F32),
        compiler_params=_params("parallel"),
        name="out_proj",
    )(o_a, o_b, merge_gates, merge_gates, x2d, w_oa, w_ob, w_out, post_gain.reshape(1, d))


def _sequence_mix(x2d, q, k16, v16, logf, glr, qkv_gla, gate_f, gate_g, merge_gates, batch, length, dims,
                  w_a2, b_a, gla_gain, w_oa, w_ob, w_out, post_g, past):
    fox_heads, head_dim, gla_heads, dk, dv = dims
    logf3 = logf.reshape(batch, length, fox_heads)
    if past is None:
        c_t = cumsum_lanes(jnp.transpose(logf3, (0, 2, 1)))
        c = jnp.transpose(c_t, (0, 2, 1)).reshape(batch * length, fox_heads)
        k_aug = key_augment(k16, c, fox_heads, head_dim)
        o_a = fox_prompt(q, k_aug, v16, c_t, gate_f, batch, length, fox_heads, head_dim)
        s0 = None
    else:
        cache_k, cache_v, cache_logf, s0, layer = past
        n_past = cache_logf.shape[1]
        logf_all = jnp.concatenate([cache_logf.astype(F32), logf3], axis=1)
        c_t = cumsum_lanes(jnp.transpose(logf_all, (0, 2, 1)))
        c = jnp.transpose(c_t[:, :, n_past:], (0, 2, 1)).reshape(batch * length, fox_heads)
        o_a = fox_cached(q, cache_k, cache_v, layer, k16, v16, c_t, c, gate_f, batch, length, fox_heads, head_dim)
    o_b, s_new = gla(qkv_gla, glr, w_a2, b_a, gla_gain, gate_g, s0, batch, length, gla_heads, dk, dv)
    y = out_proj(o_a, o_b, merge_gates, x2d, w_oa, w_ob, w_out, post_g)
    return y, logf3, s_new


def kernel(x_prompt, x_sample, cache_k, cache_v, cache_logf, state_gla, w_in, w_a2, b_a, b_f, gla_gain, w_oa, w_ob, w_out, pre_norm, post_norm):
    depth, d_model, n_in = w_in.shape
    bp, lp, _ = x_prompt.shape
    bs, ls, _ = x_sample.shape
    n_past, fox_heads, head_dim = cache_k.shape[2], cache_k.shape[3], cache_k.shape[4]
    gla_heads, dk, dv = state_gla.shape[2], state_gla.shape[3], state_gla.shape[4]
    rank = w_a2.shape[1]
    dims = (fox_heads, head_dim, gla_heads, dk, dv)
    fw, gw = fox_heads * head_dim, gla_heads * dk
    sizes = (fw, fw, fw, fox_heads, fw, gw, gw, gw, rank, gw, d_model, d_model)
    offs = [0]
    for s in sizes:
        offs.append(offs[-1] + s)
    o_fq, o_fk, o_fv, o_ff, o_fg, o_gq, _, _, o_glr, o_gg, o_ma, _, _ = offs
    assert offs[-1] == n_in and gla_heads * dv == gw

    full = (n_in // LANES) * LANES
    w_tail = jnp.pad(w_in[:, :, full:], ((0, 0), (0, 0), (0, LANES - (n_in - full))))
    cache_k2 = cache_k.reshape(depth, bs, n_past * fox_heads, head_dim)
    cache_v2 = cache_v.reshape(depth, bs, n_past * fox_heads, head_dim)
    w_oa16, w_ob16, w_out16 = w_oa.astype(BF16), w_ob.astype(BF16), w_out.astype(BF16)

    yp = x_prompt.reshape(bp * lp, d_model)
    ys = x_sample.reshape(bs * ls, d_model)
    outs_p, outs_s = [], []
    for l in range(depth):
        hp = rmsnorm_bf16(yp, pre_norm[l])
        hs = rmsnorm_bf16(ys, pre_norm[l])
        P = functools.partial(proj, hp, hs, w_in, w_tail, l)
        (q_p,), (q_s,) = P(o_fq, fw, fw, "scale", head_dim ** -0.5 * LOG2E)
        (k32_p, k16_p), (k32_s, k16_s) = P(o_fk, fw, fw, "dual")
        (v32_p, v16_p), (v32_s, v16_s) = P(o_fv, fw, fw, "dual")
        (gf_p,), (gf_s,) = P(o_fg, fw, fw, "silu")
        (gla_p,), (gla_s,) = P(o_gq, 3 * gw, gw, "scale_first", dk ** -0.5)
        (gg_p,), (gg_s,) = P(o_gg, gw, gw, "silu")
        (mg_p,), (mg_s,) = P(o_ma, 2 * d_model, d_model, "sigmoid")
        w_small = jnp.concatenate([w_in[l, :, o_ff:o_ff + fox_heads], w_in[l, :, o_glr:o_glr + rank]], axis=1).astype(BF16)
        logf_p, glr_p = proj_small(hp, w_small, b_f[l])
        logf_s, glr_s = proj_small(hs, w_small, b_f[l])
        shared = (dims, w_a2[l], b_a[l], gla_gain[l], w_oa16[l], w_ob16[l], w_out16[l], post_norm[l])
        yp, logf3_p, s_p = _sequence_mix(yp, q_p, k16_p, v16_p, logf_p, glr_p, gla_p, gf_p, gg_p, mg_p, bp, lp,
                                         *shared, past=None)
        ys, logf3_s, s_s = _sequence_mix(ys, q_s, k16_s, v16_s, logf_s, glr_s, gla_s, gf_s, gg_s, mg_s, bs, ls,
                                         *shared, past=(cache_k2, cache_v2, cache_logf[l], state_gla[l], l))
        kv_p, kv_s = (bp, lp, fox_heads, head_dim), (bs, ls, fox_heads, head_dim)
        outs_p.append((k32_p.reshape(kv_p), v32_p.reshape(kv_p), logf3_p, s_p))
        outs_s.append((k32_s.reshape(kv_s), v32_s.reshape(kv_s), logf3_s, s_s))
    stack = lambda outs, i: jnp.stack([o[i] for o in outs])
    return (yp.reshape(bp, lp, d_model), ys.reshape(bs, ls, d_model),
            stack(outs_p, 0), stack(outs_p, 1), stack(outs_p, 2), stack(outs_p, 3),
            stack(outs_s, 0), stack(outs_s, 1), stack(outs_s, 2), stack(outs_s, 3))
```

```python
import functools

import jax
import jax.numpy as jnp
from jax import lax
from jax.experimental import pallas as pl
from jax.experimental.pallas import tpu as pltpu

GLA_CHUNK = 64
GLA_GATE_TAU = 16.0
NORM_EPS = 1e-6
GLA_SAFE_LOG_DECAY = -80.0

VMEM_LIMIT_BYTES = 56 * 1024 * 1024
F32 = jnp.float32
BF16 = jnp.bfloat16


def _params(*semantics):
    return pltpu.CompilerParams(dimension_semantics=semantics, vmem_limit_bytes=VMEM_LIMIT_BYTES)


def _tile(n, pref):
    if n <= pref:
        return n
    t = pref
    while n % t:
        t //= 2
    return t


def _sigmoid(x):
    return 1.0 / (1.0 + jnp.exp(-x))


def _log_sigmoid(x):
    return jnp.minimum(x, 0.0) - jnp.log1p(jnp.exp(-jnp.abs(x)))


def _dot_nt(a, b):
    return lax.dot_general(a, b, (((1,), (1,)), ((), ())), preferred_element_type=F32)


def _dot_tn(a, b):
    return lax.dot_general(a, b, (((0,), (0,)), ((), ())), preferred_element_type=F32)


def _split3(x):
    hi = x.astype(BF16)
    r1 = x - hi.astype(F32)
    mid = r1.astype(BF16)
    lo = (r1 - mid.astype(F32)).astype(BF16)
    return hi, mid, lo


def _rmsnorm_kernel(x_ref, g_ref, o_ref):
    x = x_ref[...]
    ms = jnp.mean(x * x, axis=-1, keepdims=True)
    o_ref[...] = (x * lax.rsqrt(ms + NORM_EPS) * g_ref[...]).astype(o_ref.dtype)


def rmsnorm_bf16(x2d, gain):
    m, d = x2d.shape
    tm = _tile(m, 512)
    return pl.pallas_call(
        _rmsnorm_kernel,
        grid=(m // tm,),
        in_specs=[pl.BlockSpec((tm, d), lambda i: (i, 0)), pl.BlockSpec((1, d), lambda i: (0, 0))],
        out_specs=pl.BlockSpec((tm, d), lambda i: (i, 0)),
        out_shape=jax.ShapeDtypeStruct((m, d), BF16),
        compiler_params=_params("parallel"),
        name="rmsnorm_bf16",
    )(x2d, gain.reshape(1, d))


LANES = 128
PROJ_TN = 1024
PROJ_TM = 512
PROJ_TM_BF16 = 1024
CAST_ROWS = 256


def _epilogue(acc, o_refs, kind, scale, col_tile, first_tiles):
    if kind == "scale":
        o_refs[0][...] = (acc * scale).astype(BF16)
    elif kind == "scale_first":
        o_refs[0][...] = (acc * jnp.where(col_tile < first_tiles, scale, 1.0)).astype(BF16)
    elif kind == "dual":
        o_refs[0][...] = acc
        o_refs[1][...] = acc.astype(BF16)
    elif kind == "silu":
        o_refs[0][...] = (acc * _sigmoid(acc)).astype(BF16)
    elif kind == "sigmoid":
        o_refs[0][...] = _sigmoid(acc).astype(BF16)
    else:
        raise ValueError(kind)


def _proj_kernel(*refs, kind, scale, shift, n_w, uses_tail, n_out, mp_tiles, first_tiles):
    hp_ref, hs_ref = refs[:2]
    w_refs = refs[2:2 + n_w]
    pos = 2 + n_w
    tail_ref = refs[pos] if uses_tail else None
    pos += int(uses_tail)
    outs_p = refs[pos:pos + n_out]
    outs_s = refs[pos + n_out:pos + 2 * n_out]
    w_sc = refs[pos + 2 * n_out]
    n = pl.program_id(0)
    m = pl.program_id(1)

    @pl.when(m == 0)
    def _():
        tn = w_sc.shape[1]
        for r0 in range(0, w_sc.shape[0], CAST_ROWS):
            tiles = [r[r0:r0 + CAST_ROWS, :] for r in w_refs]
            if uses_tail:
                tiles[-1] = jnp.where(n == pl.num_programs(0) - 1, tail_ref[r0:r0 + CAST_ROWS, :], tiles[-1])
            w = jnp.concatenate(tiles, axis=1)
            w_sc[r0:r0 + CAST_ROWS, :] = w[:, shift:shift + tn].astype(BF16)

    @pl.when(m < mp_tiles)
    def _():
        _epilogue(jnp.dot(hp_ref[...], w_sc[...], preferred_element_type=F32), outs_p, kind, scale, n, first_tiles)

    @pl.when(m == mp_tiles)
    def _():
        _epilogue(jnp.dot(hs_ref[...], w_sc[...], preferred_element_type=F32), outs_s, kind, scale, n, first_tiles)


def proj(hp, hs, w_in, w_tail, layer, col0, n_cols, seg_cols, kind, scale=1.0):
    mp, k = hp.shape
    ms = hs.shape[0]
    tn = _tile(seg_cols, PROJ_TN)
    tm = _tile(mp, PROJ_TM if kind == "dual" else PROJ_TM_BF16)
    assert n_cols % seg_cols == 0 and tn % LANES == 0
    n_tiles, mp_tiles = n_cols // tn, mp // tm
    shift, base0 = col0 % LANES, col0 // LANES
    w_per_tile = tn // LANES
    n_w = w_per_tile + (1 if shift else 0)
    full_tiles = w_in.shape[2] // LANES
    last_needed = base0 + w_per_tile * (n_tiles - 1) + n_w - 1
    assert last_needed <= full_tiles
    uses_tail = last_needed == full_tiles
    out_dtypes = (F32, BF16) if kind == "dual" else (BF16,)
    n_out = len(out_dtypes)

    def w_spec(t):
        return pl.BlockSpec((None, k, LANES),
                            lambda n, m: (layer, 0, jnp.minimum(base0 + w_per_tile * n + t, full_tiles - 1)))

    in_specs = ([pl.BlockSpec((tm, k), lambda n, m: (jnp.minimum(m, mp_tiles - 1), 0)),
                 pl.BlockSpec((ms, k), lambda n, m: (0, 0))]
                + [w_spec(t) for t in range(n_w)])
    args = [hp, hs] + [w_in] * n_w
    if uses_tail:
        in_specs.append(pl.BlockSpec((None, k, LANES), lambda n, m: (layer, 0, 0)))
        args.append(w_tail)
    out = pl.pallas_call(
        functools.partial(_proj_kernel, kind=kind, scale=scale, shift=shift, n_w=n_w, uses_tail=uses_tail,
                          n_out=n_out, mp_tiles=mp_tiles, first_tiles=seg_cols // tn),
        grid=(n_tiles, mp_tiles + 1),
        in_specs=in_specs,
        out_specs=([pl.BlockSpec((tm, tn), lambda n, m: (jnp.minimum(m, mp_tiles - 1), n)) for _ in out_dtypes]
                   + [pl.BlockSpec((ms, tn), lambda n, m: (0, n)) for _ in out_dtypes]),
        out_shape=([jax.ShapeDtypeStruct((mp, n_cols), dt) for dt in out_dtypes]
                   + [jax.ShapeDtypeStruct((ms, n_cols), dt) for dt in out_dtypes]),
        scratch_shapes=[pltpu.VMEM((k, tn), BF16)],
        compiler_params=_params("arbitrary", "arbitrary"),
        name="proj_" + kind,
    )(*args)
    return tuple(out[:n_out]), tuple(out[n_out:])


def _proj_small_kernel(h_ref, wf_ref, wr_ref, bf_ref, logf_ref, glr_ref, *, f0, n_f, r0, n_r):
    w = jnp.concatenate([wf_ref[...], wr_ref[...]], axis=1).astype(BF16)
    acc = jnp.dot(h_ref[...], w, preferred_element_type=F32)
    logf_ref[...] = _log_sigmoid(acc[:, f0:f0 + n_f] + bf_ref[...])
    glr_ref[...] = acc[:, LANES + r0:LANES + r0 + n_r]


def proj_small(h, w_in, layer, col_f, col_r, n_r, b_f):
    m, k = h.shape
    n_f = b_f.shape[0]
    f0, r0 = col_f % LANES, col_r % LANES
    assert f0 + n_f <= LANES and r0 + n_r <= LANES
    tm = _tile(m, 512)
    return pl.pallas_call(
        functools.partial(_proj_small_kernel, f0=f0, n_f=n_f, r0=r0, n_r=n_r),
        grid=(m // tm,),
        in_specs=[pl.BlockSpec((tm, k), lambda i: (i, 0)),
                  pl.BlockSpec((None, k, LANES), lambda i: (layer, 0, col_f // LANES)),
                  pl.BlockSpec((None, k, LANES), lambda i: (layer, 0, col_r // LANES)),
                  pl.BlockSpec((1, n_f), lambda i: (0, 0))],
        out_specs=[pl.BlockSpec((tm, n_f), lambda i: (i, 0)), pl.BlockSpec((tm, n_r), lambda i: (i, 0))],
        out_shape=[jax.ShapeDtypeStruct((m, n_f), F32), jax.ShapeDtypeStruct((m, n_r), F32)],
        compiler_params=_params("parallel"),
        name="proj_small",
    )(h, w_in, w_in, b_f.reshape(1, n_f))


CUMSUM_LANES = 256


def _cumsum_kernel(x_ref, o_ref, *, length):
    rows = x_ref.shape[0]
    carry = jnp.zeros((rows, 1), F32)
    for start in range(0, length, CUMSUM_LANES):
        w = min(CUMSUM_LANES, length - start)
        r = lax.broadcasted_iota(jnp.int32, (w, w), 0)
        c = lax.broadcasted_iota(jnp.int32, (w, w), 1)
        upper = (r <= c).astype(BF16)
        hi, mid, lo = _split3(x_ref[:, start:start + w])
        dot = lambda a: jnp.dot(a, upper, preferred_element_type=F32)
        seg = (dot(lo) + dot(mid)) + dot(hi) + carry
        o_ref[:, start:start + w] = seg
        carry = seg[:, w - 1:w]


def cumsum_lanes(x):
    b, h, length = x.shape
    return pl.pallas_call(
        functools.partial(_cumsum_kernel, length=length),
        grid=(b,),
        in_specs=[pl.BlockSpec((None, h, length), lambda i: (i, 0, 0))],
        out_specs=pl.BlockSpec((None, h, length), lambda i: (i, 0, 0)),
        out_shape=jax.ShapeDtypeStruct((b, h, length), F32),
        compiler_params=_params("parallel"),
        name="cumsum_logf",
    )(x)


LOG2E = 1.4426950408889634
N_SPLIT = 3


def _kaug_kernel(k_ref, c_ref, o_ref, *, heads, dh):
    hi, mid, lo = (piece.astype(F32) for piece in _split3(c_ref[...] * LOG2E))
    lane = lax.broadcasted_iota(jnp.int32, (hi.shape[0], dh), 1)
    for h in range(heads):
        extra = jnp.where(lane == 0, hi[:, h:h + 1],
                          jnp.where(lane == 1, mid[:, h:h + 1], jnp.where(lane == 2, lo[:, h:h + 1], 0.0)))
        o_ref[:, 2 * h * dh:(2 * h + 1) * dh] = k_ref[:, h * dh:(h + 1) * dh]
        o_ref[:, (2 * h + 1) * dh:(2 * h + 2) * dh] = extra.astype(BF16)


def key_augment(k16, c, heads, dh):
    m = k16.shape[0]
    tm = _tile(m, 512)
    return pl.pallas_call(
        functools.partial(_kaug_kernel, heads=heads, dh=dh),
        grid=(m // tm,),
        in_specs=[pl.BlockSpec((tm, heads * dh), lambda i: (i, 0)), pl.BlockSpec((tm, heads), lambda i: (i, 0))],
        out_specs=pl.BlockSpec((tm, 2 * heads * dh), lambda i: (i, 0)),
        out_shape=jax.ShapeDtypeStruct((m, 2 * heads * dh), BF16),
        compiler_params=_params("parallel"),
        name="key_augment",
    )(k16, c)


def _fox_prompt_kernel(q_ref, ka_ref, v_ref, ct_ref, g_ref, o_ref, m_sc, l_sc, acc_sc, sa_sc, sb_sc, mxa_sc, mxb_sc,
                       *, t, n_tiles):
    h = pl.program_id(1)
    i = pl.program_id(2)
    dh = q_ref.shape[1]
    lane = lax.broadcasted_iota(jnp.int32, (t, dh), 1)
    q_aug = jnp.concatenate([q_ref[...], jnp.where(lane < N_SPLIT, -1.0, 0.0).astype(BF16)], axis=1)
    cq = ct_ref[pl.ds(h, 1), pl.ds(pl.multiple_of(i * t, t), t)] * LOG2E
    m_sc[...] = jnp.full(m_sc.shape, -jnp.inf, F32)
    l_sc[...] = jnp.zeros(l_sc.shape, F32)
    acc_sc[...] = jnp.zeros(acc_sc.shape, F32)
    key = lax.broadcasted_iota(jnp.int32, (t, t), 0)
    qry = lax.broadcasted_iota(jnp.int32, (t, t), 1)

    def rows(j):
        return pl.ds(pl.multiple_of(jnp.minimum(j, n_tiles - 1) * t, t), t)

    def scores(j, s_sc, mx_sc):
        s = _dot_nt(ka_ref[rows(j), :], q_aug)
        s_sc[...] = s
        mx_sc[...] = jnp.max(s, axis=0, keepdims=True)

    def accumulate(j, s_sc, mx_sc, masked):
        s = s_sc[...]
        if masked:
            s = jnp.where(key + (j - i) * t <= qry, s, -jnp.inf)
            m_cur = jnp.max(s, axis=0, keepdims=True)
        else:
            m_cur = mx_sc[...]
        m_prev = m_sc[...]
        m_new = jnp.maximum(m_prev, m_cur + cq)
        p = jnp.exp2(s - (m_new - cq))
        alpha = jnp.exp2(m_prev - m_new)
        l_sc[...] = alpha * l_sc[...] + jnp.sum(p, axis=0, keepdims=True)
        acc_sc[...] = alpha * acc_sc[...] + _dot_tn(v_ref[rows(j), :], p.astype(BF16))
        m_sc[...] = m_new

    def pair(p, carry):
        scores(2 * p + 1, sb_sc, mxb_sc)
        accumulate(2 * p, sa_sc, mxa_sc, False)
        scores(2 * p + 2, sa_sc, mxa_sc)
        accumulate(2 * p + 1, sb_sc, mxb_sc, False)
        return carry

    t0 = (i // 2) * 2
    scores(0, sa_sc, mxa_sc)
    lax.fori_loop(0, i // 2, pair, 0)
    scores(t0 + 1, sb_sc, mxb_sc)
    accumulate(t0, sa_sc, mxa_sc, True)
    accumulate(t0 + 1, sb_sc, mxb_sc, True)
    o = (acc_sc[...] / l_sc[...]).T
    o_ref[...] = (o * g_ref[...].astype(F32)).astype(o_ref.dtype)


def fox_prompt(q, k_aug, v, c_t, gates, batch, length, heads, head_dim):
    tq = _tile(length, 512)
    nq = length // tq
    return pl.pallas_call(
        functools.partial(_fox_prompt_kernel, t=tq, n_tiles=nq),
        grid=(batch, heads, nq),
        in_specs=[pl.BlockSpec((tq, head_dim), lambda b, h, i: (b * nq + i, h)),
                  pl.BlockSpec((length, 2 * head_dim), lambda b, h, i: (b, h)),
                  pl.BlockSpec((length, head_dim), lambda b, h, i: (b, h)),
                  pl.BlockSpec((None, heads, length), lambda b, h, i: (b, 0, 0)),
                  pl.BlockSpec((tq, head_dim), lambda b, h, i: (b * nq + i, h))],
        out_specs=pl.BlockSpec((tq, head_dim), lambda b, h, i: (b * nq + i, h)),
        out_shape=jax.ShapeDtypeStruct((batch * length, heads * head_dim), BF16),
        scratch_shapes=[pltpu.VMEM((1, tq), F32), pltpu.VMEM((1, tq), F32), pltpu.VMEM((head_dim, tq), F32),
                        pltpu.VMEM((tq, tq), F32), pltpu.VMEM((tq, tq), F32),
                        pltpu.VMEM((1, tq), F32), pltpu.VMEM((1, tq), F32)],
        compiler_params=_params("parallel", "parallel", "arbitrary"),
        name="fox_prompt",
    )(q, k_aug, v, c_t, gates)


def _fox_cached_kernel(q_ref, kc_ref, vc_ref, kn_ref, vn_ref, ct_ref, c_ref, g_ref, o_ref, m_sc, l_sc, acc_sc,
                       *, heads, dh, lq, tk, past):
    j = pl.program_id(1)

    @pl.when(j == 0)
    def _():
        m_sc[...] = jnp.full(m_sc.shape, -jnp.inf, F32)
        l_sc[...] = jnp.zeros(l_sc.shape, F32)
        acc_sc[...] = jnp.zeros(acc_sc.shape, F32)

    cq_all = c_ref[...] * LOG2E

    def update(h, s, v):
        rows = slice(h * lq, (h + 1) * lq)
        cq = cq_all[:, h:h + 1]
        m_prev = m_sc[rows, :]
        m_new = jnp.maximum(m_prev, jnp.max(s, axis=1, keepdims=True) + cq)
        p = jnp.exp2(s - (m_new - cq))
        alpha = jnp.exp2(m_prev - m_new)
        l_sc[rows, :] = alpha * l_sc[rows, :] + jnp.sum(p, axis=1, keepdims=True)
        acc_sc[rows, :] = alpha * acc_sc[rows, :] + jnp.dot(p.astype(BF16), v, preferred_element_type=F32)
        m_sc[rows, :] = m_new

    ck_tile = ct_ref[:, pl.ds(pl.multiple_of(j * tk, tk), tk)] * LOG2E
    for h in range(heads):
        kh = kc_ref[pl.ds(h, tk, stride=heads), :].astype(BF16)
        vh = vc_ref[pl.ds(h, tk, stride=heads), :].astype(BF16)
        s = _dot_nt(q_ref[:, h * dh:(h + 1) * dh], kh) - ck_tile[h:h + 1, :]
        update(h, s, vh)

    @pl.when(j == pl.num_programs(1) - 1)
    def _():
        ck_new = ct_ref[:, past:] * LOG2E
        row = lax.broadcasted_iota(jnp.int32, (lq, lq), 0)
        col = lax.broadcasted_iota(jnp.int32, (lq, lq), 1)
        for h in range(heads):
            cols = slice(h * dh, (h + 1) * dh)
            s = _dot_nt(q_ref[:, cols], kn_ref[:, cols]) - ck_new[h:h + 1, :]
            update(h, jnp.where(col <= row, s, -jnp.inf), vn_ref[:, cols])
            rows = slice(h * lq, (h + 1) * lq)
            o = acc_sc[rows, :] / l_sc[rows, :]
            o_ref[:, cols] = (o * g_ref[:, cols].astype(F32)).astype(o_ref.dtype)


def fox_cached(q, cache_k, cache_v, layer, k_new, v_new, c_t, c, gates, batch, lq, heads, head_dim):
    past = cache_k.shape[2] // heads
    tk = _tile(past, 512)
    width = heads * head_dim
    cache_spec = pl.BlockSpec((None, None, tk * heads, head_dim), lambda b, j: (layer, b, j, 0))
    row_spec = pl.BlockSpec((lq, width), lambda b, j: (b, 0))
    return pl.pallas_call(
        functools.partial(_fox_cached_kernel, heads=heads, dh=head_dim, lq=lq, tk=tk, past=past),
        grid=(batch, past // tk),
        in_specs=[row_spec, cache_spec, cache_spec, row_spec, row_spec,
                  pl.BlockSpec((None, heads, past + lq), lambda b, j: (b, 0, 0)),
                  pl.BlockSpec((lq, heads), lambda b, j: (b, 0)),
                  row_spec],
        out_specs=row_spec,
        out_shape=jax.ShapeDtypeStruct((batch * lq, width), BF16),
        scratch_shapes=[pltpu.VMEM((heads * lq, 1), F32), pltpu.VMEM((heads * lq, 1), F32),
                        pltpu.VMEM((heads * lq, head_dim), F32)],
        compiler_params=_params("parallel", "arbitrary"),
        name="fox_cached",
    )(q, cache_k, cache_v, k_new, v_new, c_t, c, gates)


def _gla_kernel(*refs, chunk, n_chunks, has_state):
    if has_state:
        (q_ref, k_ref, v_ref, glr_ref, wa_ref, ba_ref, gain_ref, gate_ref, s0_ref,
         o_ref, s_out_ref, st_sc, a_sc, q_sc, k_sc, b_sc) = refs
    else:
        (q_ref, k_ref, v_ref, glr_ref, wa_ref, ba_ref, gain_ref, gate_ref,
         o_ref, s_out_ref, st_sc, a_sc, q_sc, k_sc, b_sc) = refs
        s0_ref = None
    step = pl.program_id(2)

    @pl.when(step == 0)
    def _():
        if has_state:
            st_sc[...] = s0_ref[...].T
        else:
            st_sc[...] = jnp.zeros(st_sc.shape, F32)

    x = jnp.dot(glr_ref[...].astype(BF16), wa_ref[...].astype(BF16), preferred_element_type=F32) + ba_ref[...]
    log_a = _log_sigmoid(x) / GLA_GATE_TAU
    safe = jnp.min(log_a) * chunk >= GLA_SAFE_LOG_DECAY

    r = lax.broadcasted_iota(jnp.int32, (chunk, chunk), 0)
    c = lax.broadcasted_iota(jnp.int32, (chunk, chunk), 1)
    causal = c <= r
    tril = causal.astype(BF16)

    def exact_scores(q, k, b):
        q_sc[...] = q
        k_sc[...] = k
        b_sc[...] = b
        t_idx = lax.broadcasted_iota(jnp.int32, (chunk, 1), 0)
        s_lane = lax.broadcasted_iota(jnp.int32, (1, chunk), 1)

        def body(s, a):
            bs = b_sc[pl.ds(s, 1), :]
            ks = k_sc[pl.ds(s, 1), :]
            d = jnp.where(t_idx >= s, b_sc[...] - bs, -jnp.inf)
            col_s = jnp.sum(q_sc[...] * jnp.exp(d) * ks, axis=1, keepdims=True)
            return a + col_s * (s_lane == s).astype(F32)

        return lax.fori_loop(0, chunk, body, jnp.zeros((chunk, chunk), F32))

    for ci in range(n_chunks):
        rows = slice(ci * chunk, (ci + 1) * chunk)
        g_hi, g_mid, g_lo = _split3(log_a[rows])
        cum = lambda a: jnp.dot(tril, a, preferred_element_type=F32)
        b = (cum(g_lo) + cum(g_mid)) + cum(g_hi)
        q = q_ref[rows, :].astype(F32)
        k = k_ref[rows, :].astype(F32)
        v = v_ref[rows, :]
        b_last = b[chunk - 1:chunk, :]
        q_dec = q * jnp.exp(b)

        @pl.when(safe)
        def _():
            k_inv = k * jnp.exp(-b)
            a_sc[...] = jnp.where(causal, _dot_nt(q_dec.astype(BF16), k_inv.astype(BF16)), 0.0)

        @pl.when(jnp.logical_not(safe))
        def _():
            a_sc[...] = exact_scores(q, k, b)

        st = st_sc[...]
        o = (jnp.dot(a_sc[...].astype(BF16), v, preferred_element_type=F32)
             + _dot_nt(q_dec.astype(BF16), st.astype(BF16)))
        k_end = k * jnp.exp(b_last - b)
        st_sc[...] = st * jnp.exp(b_last) + _dot_tn(v, k_end.astype(BF16))
        o = o * lax.rsqrt(jnp.mean(o * o, axis=-1, keepdims=True) + NORM_EPS) * gain_ref[...]
        o_ref[rows, :] = (o * gate_ref[rows, :].astype(F32)).astype(o_ref.dtype)

    @pl.when(step == pl.num_programs(2) - 1)
    def _():
        s_out_ref[...] = st_sc[...].T


def gla(qkv, glr, w_a2, b_a, gain, gates, s0, batch, length, heads, dk, dv):
    assert dk == dv
    chunk = GLA_CHUNK if length % GLA_CHUNK == 0 else length
    rows = max(_tile(length, 512), chunk)
    n_steps = length // rows
    rank = glr.shape[1]
    has_state = s0 is not None
    row_map = lambda off: (lambda b, h, i: (b * n_steps + i, off + h))
    in_specs = [pl.BlockSpec((rows, dk), row_map(0)),
                pl.BlockSpec((rows, dk), row_map(heads)),
                pl.BlockSpec((rows, dv), row_map(2 * heads)),
                pl.BlockSpec((rows, rank), lambda b, h, i: (b * n_steps + i, 0)),
                pl.BlockSpec((rank, dk), lambda b, h, i: (0, h)),
                pl.BlockSpec((1, dk), lambda b, h, i: (0, h)),
                pl.BlockSpec((1, dv), lambda b, h, i: (0, 0)),
                pl.BlockSpec((rows, dv), row_map(0))]
    args = [qkv, qkv, qkv, glr, w_a2, b_a.reshape(1, -1), gain.reshape(1, dv), gates]
    if has_state:
        in_specs.append(pl.BlockSpec((None, None, dk, dv), lambda b, h, i: (b, h, 0, 0)))
        args.append(s0)
    return pl.pallas_call(
        functools.partial(_gla_kernel, chunk=chunk, n_chunks=rows // chunk, has_state=has_state),
        grid=(batch, heads, n_steps),
        in_specs=in_specs,
        out_specs=[pl.BlockSpec((rows, dv), row_map(0)),
                   pl.BlockSpec((None, None, dk, dv), lambda b, h, i: (b, h, 0, 0))],
        out_shape=[jax.ShapeDtypeStruct((batch * length, heads * dv), BF16),
                   jax.ShapeDtypeStruct((batch, heads, dk, dv), F32)],
        scratch_shapes=[pltpu.VMEM((dv, dk), F32), pltpu.VMEM((chunk, chunk), F32),
                        pltpu.VMEM((chunk, dk), F32), pltpu.VMEM((chunk, dk), F32), pltpu.VMEM((chunk, dk), F32)],
        compiler_params=_params("parallel", "parallel", "arbitrary"),
        name="gla",
    )(*args)


def _out_kernel(oa_ref, ob_ref, sa_ref, sb_ref, x_ref, woa_ref, wob_ref, wout_ref, g_ref, y_ref):
    a = jnp.dot(oa_ref[...], woa_ref[...], preferred_element_type=F32)
    b = jnp.dot(ob_ref[...], wob_ref[...], preferred_element_type=F32)
    merged = sa_ref[...].astype(F32) * a + sb_ref[...].astype(F32) * b
    z = jnp.dot(merged.astype(BF16), wout_ref[...], preferred_element_type=F32)
    zn = z * lax.rsqrt(jnp.mean(z * z, axis=-1, keepdims=True) + NORM_EPS) * g_ref[...]
    y_ref[...] = x_ref[...] + zn


def out_proj(o_a, o_b, merge_gates, x2d, w_oa, w_ob, w_out, post_gain):
    m, d = x2d.shape
    wa, wb = o_a.shape[1], o_b.shape[1]
    tm = _tile(m, 256)
    resident = lambda shape: pl.BlockSpec(shape, lambda i: (0, 0), pipeline_mode=pl.Buffered(1))
    return pl.pallas_call(
        _out_kernel,
        grid=(m // tm,),
        in_specs=[pl.BlockSpec((tm, wa), lambda i: (i, 0)),
                  pl.BlockSpec((tm, wb), lambda i: (i, 0)),
                  pl.BlockSpec((tm, d), lambda i: (i, 0)),
                  pl.BlockSpec((tm, d), lambda i: (i, 1)),
                  pl.BlockSpec((tm, d), lambda i: (i, 0)),
                  resident((wa, d)), resident((wb, d)), resident((d, d)), resident((1, d))],
        out_specs=pl.BlockSpec((tm, d), lambda i: (i, 0)),
        out_shape=jax.ShapeDtypeStruct((m, d), F32),
        compiler_params=_params("parallel"),
        name="out_proj",
    )(o_a, o_b, merge_gates, merge_gates, x2d, w_oa, w_ob, w_out, post_gain.reshape(1, d))


def _sequence_mix(x2d, q, k16, v16, logf, glr, qkv_gla, gate_f, gate_g, merge_gates, batch, length, dims,
                  w_a2, b_a, gla_gain, w_oa, w_ob, w_out, post_g, past):
    fox_heads, head_dim, gla_heads, dk, dv = dims
    logf3 = logf.reshape(batch, length, fox_heads)
    if past is None:
        c_t = cumsum_lanes(jnp.transpose(logf3, (0, 2, 1)))
        c = jnp.transpose(c_t, (0, 2, 1)).reshape(batch * length, fox_heads)
        k_aug = key_augment(k16, c, fox_heads, head_dim)
        o_a = fox_prompt(q, k_aug, v16, c_t, gate_f, batch, length, fox_heads, head_dim)
        s0 = None
    else:
        cache_k, cache_v, cache_logf, s0, layer = past
        n_past = cache_logf.shape[1]
        logf_all = jnp.concatenate([cache_logf.astype(F32), logf3], axis=1)
        c_t = cumsum_lanes(jnp.transpose(logf_all, (0, 2, 1)))
        c = jnp.transpose(c_t[:, :, n_past:], (0, 2, 1)).reshape(batch * length, fox_heads)
        o_a = fox_cached(q, cache_k, cache_v, layer, k16, v16, c_t, c, gate_f, batch, length, fox_heads, head_dim)
    o_b, s_new = gla(qkv_gla, glr, w_a2, b_a, gla_gain, gate_g, s0, batch, length, gla_heads, dk, dv)
    y = out_proj(o_a, o_b, merge_gates, x2d, w_oa, w_ob, w_out, post_g)
    return y, logf3, s_new


def kernel(x_prompt, x_sample, cache_k, cache_v, cache_logf, state_gla, w_in, w_a2, b_a, b_f, gla_gain, w_oa, w_ob, w_out, pre_norm, post_norm):
    depth, d_model, n_in = w_in.shape
    bp, lp, _ = x_prompt.shape
    bs, ls, _ = x_sample.shape
    n_past, fox_heads, head_dim = cache_k.shape[2], cache_k.shape[3], cache_k.shape[4]
    gla_heads, dk, dv = state_gla.shape[2], state_gla.shape[3], state_gla.shape[4]
    rank = w_a2.shape[1]
    dims = (fox_heads, head_dim, gla_heads, dk, dv)
    fw, gw = fox_heads * head_dim, gla_heads * dk
    sizes = (fw, fw, fw, fox_heads, fw, gw, gw, gw, rank, gw, d_model, d_model)
    offs = [0]
    for s in sizes:
        offs.append(offs[-1] + s)
    o_fq, o_fk, o_fv, o_ff, o_fg, o_gq, _, _, o_glr, o_gg, o_ma, _, _ = offs
    assert offs[-1] == n_in and gla_heads * dv == gw

    full = (n_in // LANES) * LANES
    w_tail = jnp.pad(w_in[:, :, full:], ((0, 0), (0, 0), (0, LANES - (n_in - full))))
    cache_k2 = cache_k.reshape(depth, bs, n_past * fox_heads, head_dim)
    cache_v2 = cache_v.reshape(depth, bs, n_past * fox_heads, head_dim)
    w_oa16, w_ob16, w_out16 = w_oa.astype(BF16), w_ob.astype(BF16), w_out.astype(BF16)

    yp = x_prompt.reshape(bp * lp, d_model)
    ys = x_sample.reshape(bs * ls, d_model)
    outs_p, outs_s = [], []
    for l in range(depth):
        hp = rmsnorm_bf16(yp, pre_norm[l])
        hs = rmsnorm_bf16(ys, pre_norm[l])
        P = functools.partial(proj, hp, hs, w_in, w_tail, l)
        (q_p,), (q_s,) = P(o_fq, fw, fw, "scale", head_dim ** -0.5 * LOG2E)
        (k32_p, k16_p), (k32_s, k16_s) = P(o_fk, fw, fw, "dual")
        (v32_p, v16_p), (v32_s, v16_s) = P(o_fv, fw, fw, "dual")
        (gf_p,), (gf_s,) = P(o_fg, fw, fw, "silu")
        (gla_p,), (gla_s,) = P(o_gq, 3 * gw, gw, "scale_first", dk ** -0.5)
        (gg_p,), (gg_s,) = P(o_gg, gw, gw, "silu")
        (mg_p,), (mg_s,) = P(o_ma, 2 * d_model, d_model, "sigmoid")
        logf_p, glr_p = proj_small(hp, w_in, l, o_ff, o_glr, rank, b_f[l])
        logf_s, glr_s = proj_small(hs, w_in, l, o_ff, o_glr, rank, b_f[l])
        shared = (dims, w_a2[l], b_a[l], gla_gain[l], w_oa16[l], w_ob16[l], w_out16[l], post_norm[l])
        yp, logf3_p, s_p = _sequence_mix(yp, q_p, k16_p, v16_p, logf_p, glr_p, gla_p, gf_p, gg_p, mg_p, bp, lp,
                                         *shared, past=None)
        ys, logf3_s, s_s = _sequence_mix(ys, q_s, k16_s, v16_s, logf_s, glr_s, gla_s, gf_s, gg_s, mg_s, bs, ls,
                                         *shared, past=(cache_k2, cache_v2, cache_logf[l], state_gla[l], l))
        kv_p, kv_s = (bp, lp, fox_heads, head_dim), (bs, ls, fox_heads, head_dim)
        outs_p.append((k32_p.reshape(kv_p), v32_p.reshape(kv_p), logf3_p, s_p))
        outs_s.append((k32_s.reshape(kv_s), v32_s.reshape(kv_s), logf3_s, s_s))
    stack = lambda outs, i: jnp.stack([o[i] for o in outs])
    return (yp.reshape(bp, lp, d_model), ys.reshape(bs, ls, d_model),
            stack(outs_p, 0), stack(outs_p, 1), stack(outs_p, 2), stack(outs_p, 3),
            stack(outs_s, 0), stack(outs_s, 1), stack(outs_s, 2), stack(outs_s, 3))
```

```python
import functools

import jax
import jax.numpy as jnp
from jax import lax
from jax.experimental import pallas as pl
from jax.experimental.pallas import tpu as pltpu

GLA_CHUNK = 64
GLA_GATE_TAU = 16.0
NORM_EPS = 1e-6
GLA_SAFE_LOG_DECAY = -80.0

VMEM_LIMIT_BYTES = 56 * 1024 * 1024
F32 = jnp.float32
BF16 = jnp.bfloat16


def _params(*semantics):
    return pltpu.CompilerParams(dimension_semantics=semantics, vmem_limit_bytes=VMEM_LIMIT_BYTES)


def _tile(n, pref):
    if n <= pref:
        return n
    t = pref
    while n % t:
        t //= 2
    return t


def _sigmoid(x):
    return 1.0 / (1.0 + jnp.exp(-x))


def _log_sigmoid(x):
    return jnp.minimum(x, 0.0) - jnp.log1p(jnp.exp(-jnp.abs(x)))


def _dot_nt(a, b):
    return lax.dot_general(a, b, (((1,), (1,)), ((), ())), preferred_element_type=F32)


def _dot_tn(a, b):
    return lax.dot_general(a, b, (((0,), (0,)), ((), ())), preferred_element_type=F32)


def _split3(x):
    hi = x.astype(BF16)
    r1 = x - hi.astype(F32)
    mid = r1.astype(BF16)
    lo = (r1 - mid.astype(F32)).astype(BF16)
    return hi, mid, lo


def _rmsnorm_kernel(x_ref, g_ref, o_ref):
    x = x_ref[...]
    ms = jnp.mean(x * x, axis=-1, keepdims=True)
    o_ref[...] = (x * lax.rsqrt(ms + NORM_EPS) * g_ref[...]).astype(o_ref.dtype)


def rmsnorm_bf16(x2d, gain):
    m, d = x2d.shape
    tm = _tile(m, 512)
    return pl.pallas_call(
        _rmsnorm_kernel,
        grid=(m // tm,),
        in_specs=[pl.BlockSpec((tm, d), lambda i: (i, 0)), pl.BlockSpec((1, d), lambda i: (0, 0))],
        out_specs=pl.BlockSpec((tm, d), lambda i: (i, 0)),
        out_shape=jax.ShapeDtypeStruct((m, d), BF16),
        compiler_params=_params("parallel"),
        name="rmsnorm_bf16",
    )(x2d, gain.reshape(1, d))


LANES = 128
PROJ_TN = 1024
PROJ_TM = 512
PROJ_TM_BF16 = 1024
CAST_ROWS = 256


def _epilogue(acc, o_refs, kind, scale, col_tile, first_tiles):
    if kind == "scale":
        o_refs[0][...] = (acc * scale).astype(BF16)
    elif kind == "scale_first":
        o_refs[0][...] = (acc * jnp.where(col_tile < first_tiles, scale, 1.0)).astype(BF16)
    elif kind == "dual":
        o_refs[0][...] = acc
        o_refs[1][...] = acc.astype(BF16)
    elif kind == "silu":
        o_refs[0][...] = (acc * _sigmoid(acc)).astype(BF16)
    elif kind == "sigmoid":
        o_refs[0][...] = _sigmoid(acc).astype(BF16)
    else:
        raise ValueError(kind)


def _proj_kernel(*refs, kind, scale, shift, n_w, uses_tail, n_out, mp_tiles, first_tiles):
    hp_ref, hs_ref = refs[:2]
    w_refs = refs[2:2 + n_w]
    pos = 2 + n_w
    tail_ref = refs[pos] if uses_tail else None
    pos += int(uses_tail)
    outs_p = refs[pos:pos + n_out]
    outs_s = refs[pos + n_out:pos + 2 * n_out]
    w_sc = refs[pos + 2 * n_out]
    n = pl.program_id(0)
    m = pl.program_id(1)

    @pl.when(m == 0)
    def _():
        tn = w_sc.shape[1]
        for r0 in range(0, w_sc.shape[0], CAST_ROWS):
            tiles = [r[r0:r0 + CAST_ROWS, :] for r in w_refs]
            if uses_tail:
                tiles[-1] = jnp.where(n == pl.num_programs(0) - 1, tail_ref[r0:r0 + CAST_ROWS, :], tiles[-1])
            w = jnp.concatenate(tiles, axis=1)
            w_sc[r0:r0 + CAST_ROWS, :] = w[:, shift:shift + tn].astype(BF16)

    @pl.when(m < mp_tiles)
    def _():
        _epilogue(jnp.dot(hp_ref[...], w_sc[...], preferred_element_type=F32), outs_p, kind, scale, n, first_tiles)

    @pl.when(m == mp_tiles)
    def _():
        _epilogue(jnp.dot(hs_ref[...], w_sc[...], preferred_element_type=F32), outs_s, kind, scale, n, first_tiles)


def proj(hp, hs, w_in, w_tail, layer, col0, n_cols, seg_cols, kind, scale=1.0):
    mp, k = hp.shape
    ms = hs.shape[0]
    tn = _tile(seg_cols, PROJ_TN)
    tm = _tile(mp, PROJ_TM if kind == "dual" else PROJ_TM_BF16)
    assert n_cols % seg_cols == 0 and tn % LANES == 0
    n_tiles, mp_tiles = n_cols // tn, mp // tm
    shift, base0 = col0 % LANES, col0 // LANES
    w_per_tile = tn // LANES
    n_w = w_per_tile + (1 if shift else 0)
    full_tiles = w_in.shape[2] // LANES
    last_needed = base0 + w_per_tile * (n_tiles - 1) + n_w - 1
    assert last_needed <= full_tiles
    uses_tail = last_needed == full_tiles
    out_dtypes = (F32, BF16) if kind == "dual" else (BF16,)
    n_out = len(out_dtypes)

    def w_spec(t):
        return pl.BlockSpec((None, k, LANES),
                            lambda n, m: (layer, 0, jnp.minimum(base0 + w_per_tile * n + t, full_tiles - 1)))

    in_specs = ([pl.BlockSpec((tm, k), lambda n, m: (jnp.minimum(m, mp_tiles - 1), 0)),
                 pl.BlockSpec((ms, k), lambda n, m: (0, 0))]
                + [w_spec(t) for t in range(n_w)])
    args = [hp, hs] + [w_in] * n_w
    if uses_tail:
        in_specs.append(pl.BlockSpec((None, k, LANES), lambda n, m: (layer, 0, 0)))
        args.append(w_tail)
    out = pl.pallas_call(
        functools.partial(_proj_kernel, kind=kind, scale=scale, shift=shift, n_w=n_w, uses_tail=uses_tail,
                          n_out=n_out, mp_tiles=mp_tiles, first_tiles=seg_cols // tn),
        grid=(n_tiles, mp_tiles + 1),
        in_specs=in_specs,
        out_specs=([pl.BlockSpec((tm, tn), lambda n, m: (jnp.minimum(m, mp_tiles - 1), n)) for _ in out_dtypes]
                   + [pl.BlockSpec((ms, tn), lambda n, m: (0, n)) for _ in out_dtypes]),
        out_shape=([jax.ShapeDtypeStruct((mp, n_cols), dt) for dt in out_dtypes]
                   + [jax.ShapeDtypeStruct((ms, n_cols), dt) for dt in out_dtypes]),
        scratch_shapes=[pltpu.VMEM((k, tn), BF16)],
        compiler_params=_params("arbitrary", "arbitrary"),
        name="proj_" + kind,
    )(*args)
    return tuple(out[:n_out]), tuple(out[n_out:])


def _proj_small_kernel(h_ref, wf_ref, wr_ref, bf_ref, logf_ref, glr_ref, *, f0, n_f, r0, n_r):
    w = jnp.concatenate([wf_ref[...], wr_ref[...]], axis=1).astype(BF16)
    acc = jnp.dot(h_ref[...], w, preferred_element_type=F32)
    logf_ref[...] = _log_sigmoid(acc[:, f0:f0 + n_f] + bf_ref[...])
    glr_ref[...] = acc[:, LANES + r0:LANES + r0 + n_r]


def proj_small(h, w_in, layer, col_f, col_r, n_r, b_f):
    m, k = h.shape
    n_f = b_f.shape[0]
    f0, r0 = col_f % LANES, col_r % LANES
    assert f0 + n_f <= LANES and r0 + n_r <= LANES
    tm = _tile(m, 512)
    return pl.pallas_call(
        functools.partial(_proj_small_kernel, f0=f0, n_f=n_f, r0=r0, n_r=n_r),
        grid=(m // tm,),
        in_specs=[pl.BlockSpec((tm, k), lambda i: (i, 0)),
                  pl.BlockSpec((None, k, LANES), lambda i: (layer, 0, col_f // LANES)),
                  pl.BlockSpec((None, k, LANES), lambda i: (layer, 0, col_r // LANES)),
                  pl.BlockSpec((1, n_f), lambda i: (0, 0))],
        out_specs=[pl.BlockSpec((tm, n_f), lambda i: (i, 0)), pl.BlockSpec((tm, n_r), lambda i: (i, 0))],
        out_shape=[jax.ShapeDtypeStruct((m, n_f), F32), jax.ShapeDtypeStruct((m, n_r), F32)],
        compiler_params=_params("parallel"),
        name="proj_small",
    )(h, w_in, w_in, b_f.reshape(1, n_f))


CUMSUM_LANES = 256


def _cumsum_kernel(x_ref, o_ref, *, length):
    rows = x_ref.shape[0]
    carry = jnp.zeros((rows, 1), F32)
    for start in range(0, length, CUMSUM_LANES):
        w = min(CUMSUM_LANES, length - start)
        r = lax.broadcasted_iota(jnp.int32, (w, w), 0)
        c = lax.broadcasted_iota(jnp.int32, (w, w), 1)
        upper = (r <= c).astype(BF16)
        hi, mid, lo = _split3(x_ref[:, start:start + w])
        dot = lambda a: jnp.dot(a, upper, preferred_element_type=F32)
        seg = (dot(lo) + dot(mid)) + dot(hi) + carry
        o_ref[:, start:start + w] = seg
        carry = seg[:, w - 1:w]


def cumsum_lanes(x):
    b, h, length = x.shape
    return pl.pallas_call(
        functools.partial(_cumsum_kernel, length=length),
        grid=(b,),
        in_specs=[pl.BlockSpec((None, h, length), lambda i: (i, 0, 0))],
        out_specs=pl.BlockSpec((None, h, length), lambda i: (i, 0, 0)),
        out_shape=jax.ShapeDtypeStruct((b, h, length), F32),
        compiler_params=_params("parallel"),
        name="cumsum_logf",
    )(x)


LOG2E = 1.4426950408889634
N_SPLIT = 3


def _kaug_kernel(k_ref, c_ref, o_ref, *, heads, dh):
    hi, mid, lo = (piece.astype(F32) for piece in _split3(c_ref[...] * LOG2E))
    lane = lax.broadcasted_iota(jnp.int32, (hi.shape[0], dh), 1)
    for h in range(heads):
        extra = jnp.where(lane == 0, hi[:, h:h + 1],
                          jnp.where(lane == 1, mid[:, h:h + 1], jnp.where(lane == 2, lo[:, h:h + 1], 0.0)))
        o_ref[:, 2 * h * dh:(2 * h + 1) * dh] = k_ref[:, h * dh:(h + 1) * dh]
        o_ref[:, (2 * h + 1) * dh:(2 * h + 2) * dh] = extra.astype(BF16)


def key_augment(k16, c, heads, dh):
    m = k16.shape[0]
    tm = _tile(m, 512)
    return pl.pallas_call(
        functools.partial(_kaug_kernel, heads=heads, dh=dh),
        grid=(m // tm,),
        in_specs=[pl.BlockSpec((tm, heads * dh), lambda i: (i, 0)), pl.BlockSpec((tm, heads), lambda i: (i, 0))],
        out_specs=pl.BlockSpec((tm, 2 * heads * dh), lambda i: (i, 0)),
        out_shape=jax.ShapeDtypeStruct((m, 2 * heads * dh), BF16),
        compiler_params=_params("parallel"),
        name="key_augment",
    )(k16, c)


def _fox_prompt_kernel(q_ref, ka_ref, v_ref, ct_ref, g_ref, o_ref, m_sc, l_sc, acc_sc, sa_sc, sb_sc, mxa_sc, mxb_sc,
                       *, t, n_tiles):
    h = pl.program_id(1)
    i = pl.program_id(2)
    dh = q_ref.shape[1]
    lane = lax.broadcasted_iota(jnp.int32, (t, dh), 1)
    q_aug = jnp.concatenate([q_ref[...], jnp.where(lane < N_SPLIT, -1.0, 0.0).astype(BF16)], axis=1)
    cq = ct_ref[pl.ds(h, 1), pl.ds(pl.multiple_of(i * t, t), t)] * LOG2E
    m_sc[...] = jnp.full(m_sc.shape, -jnp.inf, F32)
    l_sc[...] = jnp.zeros(l_sc.shape, F32)
    acc_sc[...] = jnp.zeros(acc_sc.shape, F32)
    key = lax.broadcasted_iota(jnp.int32, (t, t), 0)
    qry = lax.broadcasted_iota(jnp.int32, (t, t), 1)

    def rows(j):
        return pl.ds(pl.multiple_of(jnp.minimum(j, n_tiles - 1) * t, t), t)

    def scores(j, s_sc, mx_sc):
        s = _dot_nt(ka_ref[rows(j), :], q_aug)
        s_sc[...] = s
        mx_sc[...] = jnp.max(s, axis=0, keepdims=True)

    def accumulate(j, s_sc, mx_sc, masked):
        s = s_sc[...]
        if masked:
            s = jnp.where(key + (j - i) * t <= qry, s, -jnp.inf)
            m_cur = jnp.max(s, axis=0, keepdims=True)
        else:
            m_cur = mx_sc[...]
        m_prev = m_sc[...]
        m_new = jnp.maximum(m_prev, m_cur + cq)
        p = jnp.exp2(s - (m_new - cq))
        alpha = jnp.exp2(m_prev - m_new)
        l_sc[...] = alpha * l_sc[...] + jnp.sum(p, axis=0, keepdims=True)
        acc_sc[...] = alpha * acc_sc[...] + _dot_tn(v_ref[rows(j), :], p.astype(BF16))
        m_sc[...] = m_new

    def pair(p, carry):
        scores(2 * p + 1, sb_sc, mxb_sc)
        accumulate(2 * p, sa_sc, mxa_sc, False)
        scores(2 * p + 2, sa_sc, mxa_sc)
        accumulate(2 * p + 1, sb_sc, mxb_sc, False)
        return carry

    t0 = (i // 2) * 2
    scores(0, sa_sc, mxa_sc)
    lax.fori_loop(0, i // 2, pair, 0)
    @pl.when(i % 2 == 1)
    def _():
        scores(t0 + 1, sb_sc, mxb_sc)
        accumulate(t0, sa_sc, mxa_sc, True)
        accumulate(t0 + 1, sb_sc, mxb_sc, True)

    @pl.when(i % 2 == 0)
    def _():
        accumulate(t0, sa_sc, mxa_sc, True)

    o = (acc_sc[...] / l_sc[...]).T
    o_ref[...] = (o * g_ref[...].astype(F32)).astype(o_ref.dtype)


def fox_prompt(q, k_aug, v, c_t, gates, batch, length, heads, head_dim):
    tq = _tile(length, 512)
    nq = length // tq
    return pl.pallas_call(
        functools.partial(_fox_prompt_kernel, t=tq, n_tiles=nq),
        grid=(batch, heads, nq),
        in_specs=[pl.BlockSpec((tq, head_dim), lambda b, h, i: (b * nq + i, h)),
                  pl.BlockSpec((length, 2 * head_dim), lambda b, h, i: (b, h)),
                  pl.BlockSpec((length, head_dim), lambda b, h, i: (b, h)),
                  pl.BlockSpec((None, heads, length), lambda b, h, i: (b, 0, 0)),
                  pl.BlockSpec((tq, head_dim), lambda b, h, i: (b * nq + i, h))],
        out_specs=pl.BlockSpec((tq, head_dim), lambda b, h, i: (b * nq + i, h)),
        out_shape=jax.ShapeDtypeStruct((batch * length, heads * head_dim), BF16),
        scratch_shapes=[pltpu.VMEM((1, tq), F32), pltpu.VMEM((1, tq), F32), pltpu.VMEM((head_dim, tq), F32),
                        pltpu.VMEM((tq, tq), F32), pltpu.VMEM((tq, tq), F32),
                        pltpu.VMEM((1, tq), F32), pltpu.VMEM((1, tq), F32)],
        compiler_params=_params("parallel", "parallel", "arbitrary"),
        name="fox_prompt",
    )(q, k_aug, v, c_t, gates)


def _fox_cached_kernel(q_ref, kc_ref, vc_ref, kn_ref, vn_ref, ct_ref, c_ref, g_ref, o_ref, m_sc, l_sc, acc_sc,
                       *, heads, dh, lq, tk, past):
    j = pl.program_id(1)

    @pl.when(j == 0)
    def _():
        m_sc[...] = jnp.full(m_sc.shape, -jnp.inf, F32)
        l_sc[...] = jnp.zeros(l_sc.shape, F32)
        acc_sc[...] = jnp.zeros(acc_sc.shape, F32)

    cq_all = c_ref[...] * LOG2E

    def update(h, s, v):
        rows = slice(h * lq, (h + 1) * lq)
        cq = cq_all[:, h:h + 1]
        m_prev = m_sc[rows, :]
        m_new = jnp.maximum(m_prev, jnp.max(s, axis=1, keepdims=True) + cq)
        p = jnp.exp2(s - (m_new - cq))
        alpha = jnp.exp2(m_prev - m_new)
        l_sc[rows, :] = alpha * l_sc[rows, :] + jnp.sum(p, axis=1, keepdims=True)
        acc_sc[rows, :] = alpha * acc_sc[rows, :] + jnp.dot(p.astype(BF16), v, preferred_element_type=F32)
        m_sc[rows, :] = m_new

    ck_tile = ct_ref[:, pl.ds(pl.multiple_of(j * tk, tk), tk)] * LOG2E
    for h in range(heads):
        kh = kc_ref[pl.ds(h, tk, stride=heads), :].astype(BF16)
        vh = vc_ref[pl.ds(h, tk, stride=heads), :].astype(BF16)
        s = _dot_nt(q_ref[:, h * dh:(h + 1) * dh], kh) - ck_tile[h:h + 1, :]
        update(h, s, vh)

    @pl.when(j == pl.num_programs(1) - 1)
    def _():
        ck_new = ct_ref[:, past:] * LOG2E
        row = lax.broadcasted_iota(jnp.int32, (lq, lq), 0)
        col = lax.broadcasted_iota(jnp.int32, (lq, lq), 1)
        for h in range(heads):
            cols = slice(h * dh, (h + 1) * dh)
            s = _dot_nt(q_ref[:, cols], kn_ref[:, cols]) - ck_new[h:h + 1, :]
            update(h, jnp.where(col <= row, s, -jnp.inf), vn_ref[:, cols])
            rows = slice(h * lq, (h + 1) * lq)
            o = acc_sc[rows, :] / l_sc[rows, :]
            o_ref[:, cols] = (o * g_ref[:, cols].astype(F32)).astype(o_ref.dtype)


def fox_cached(q, cache_k, cache_v, layer, k_new, v_new, c_t, c, gates, batch, lq, heads, head_dim):
    past = cache_k.shape[2] // heads
    tk = _tile(past, 512)
    width = heads * head_dim
    cache_spec = pl.BlockSpec((None, None, tk * heads, head_dim), lambda b, j: (layer, b, j, 0))
    row_spec = pl.BlockSpec((lq, width), lambda b, j: (b, 0))
    return pl.pallas_call(
        functools.partial(_fox_cached_kernel, heads=heads, dh=head_dim, lq=lq, tk=tk, past=past),
        grid=(batch, past // tk),
        in_specs=[row_spec, cache_spec, cache_spec, row_spec, row_spec,
                  pl.BlockSpec((None, heads, past + lq), lambda b, j: (b, 0, 0)),
                  pl.BlockSpec((lq, heads), lambda b, j: (b, 0)),
                  row_spec],
        out_specs=row_spec,
        out_shape=jax.ShapeDtypeStruct((batch * lq, width), BF16),
        scratch_shapes=[pltpu.VMEM((heads * lq, 1), F32), pltpu.VMEM((heads * lq, 1), F32),
                        pltpu.VMEM((heads * lq, head_dim), F32)],
        compiler_params=_params("parallel", "arbitrary"),
        name="fox_cached",
    )(q, cache_k, cache_v, k_new, v_new, c_t, c, gates)


def _gla_kernel(*refs, chunk, n_chunks, has_state):
    if has_state:
        (q_ref, k_ref, v_ref, glr_ref, wa_ref, ba_ref, gain_ref, gate_ref, s0_ref,
         o_ref, s_out_ref, st_sc, a_sc, q_sc, k_sc, b_sc) = refs
    else:
        (q_ref, k_ref, v_ref, glr_ref, wa_ref, ba_ref, gain_ref, gate_ref,
         o_ref, s_out_ref, st_sc, a_sc, q_sc, k_sc, b_sc) = refs
        s0_ref = None
    step = pl.program_id(2)

    @pl.when(step == 0)
    def _():
        if has_state:
            st_sc[...] = s0_ref[...].T
        else:
            st_sc[...] = jnp.zeros(st_sc.shape, F32)

    x = jnp.dot(glr_ref[...].astype(BF16), wa_ref[...].astype(BF16), preferred_element_type=F32) + ba_ref[...]
    log_a = _log_sigmoid(x) / GLA_GATE_TAU
    safe = jnp.min(log_a) * chunk >= GLA_SAFE_LOG_DECAY

    r = lax.broadcasted_iota(jnp.int32, (chunk, chunk), 0)
    c = lax.broadcasted_iota(jnp.int32, (chunk, chunk), 1)
    causal = c <= r
    tril = causal.astype(BF16)

    def exact_scores(q, k, b):
        q_sc[...] = q
        k_sc[...] = k
        b_sc[...] = b
        t_idx = lax.broadcasted_iota(jnp.int32, (chunk, 1), 0)
        s_lane = lax.broadcasted_iota(jnp.int32, (1, chunk), 1)

        def body(s, a):
            bs = b_sc[pl.ds(s, 1), :]
            ks = k_sc[pl.ds(s, 1), :]
            d = jnp.where(t_idx >= s, b_sc[...] - bs, -jnp.inf)
            col_s = jnp.sum(q_sc[...] * jnp.exp(d) * ks, axis=1, keepdims=True)
            return a + col_s * (s_lane == s).astype(F32)

        return lax.fori_loop(0, chunk, body, jnp.zeros((chunk, chunk), F32))

    for ci in range(n_chunks):
        rows = slice(ci * chunk, (ci + 1) * chunk)
        g_hi, g_mid, g_lo = _split3(log_a[rows])
        cum = lambda a: jnp.dot(tril, a, preferred_element_type=F32)
        b = (cum(g_lo) + cum(g_mid)) + cum(g_hi)
        q = q_ref[rows, :].astype(F32)
        k = k_ref[rows, :].astype(F32)
        v = v_ref[rows, :]
        b_last = b[chunk - 1:chunk, :]
        q_dec = q * jnp.exp(b)

        @pl.when(safe)
        def _():
            k_inv = k * jnp.exp(-b)
            a_sc[...] = jnp.where(causal, _dot_nt(q_dec.astype(BF16), k_inv.astype(BF16)), 0.0)

        @pl.when(jnp.logical_not(safe))
        def _():
            a_sc[...] = exact_scores(q, k, b)

        st = st_sc[...]
        o = (jnp.dot(a_sc[...].astype(BF16), v, preferred_element_type=F32)
             + _dot_nt(q_dec.astype(BF16), st.astype(BF16)))
        k_end = k * jnp.exp(b_last - b)
        st_sc[...] = st * jnp.exp(b_last) + _dot_tn(v, k_end.astype(BF16))
        o = o * lax.rsqrt(jnp.mean(o * o, axis=-1, keepdims=True) + NORM_EPS) * gain_ref[...]
        o_ref[rows, :] = (o * gate_ref[rows, :].astype(F32)).astype(o_ref.dtype)

    @pl.when(step == pl.num_programs(2) - 1)
    def _():
        s_out_ref[...] = st_sc[...].T


def gla(qkv, glr, w_a2, b_a, gain, gates, s0, batch, length, heads, dk, dv):
    assert dk == dv
    chunk = GLA_CHUNK if length % GLA_CHUNK == 0 else length
    rows = max(_tile(length, 512), chunk)
    n_steps = length // rows
    rank = glr.shape[1]
    has_state = s0 is not None
    row_map = lambda off: (lambda b, h, i: (b * n_steps + i, off + h))
    in_specs = [pl.BlockSpec((rows, dk), row_map(0)),
                pl.BlockSpec((rows, dk), row_map(heads)),
                pl.BlockSpec((rows, dv), row_map(2 * heads)),
                pl.BlockSpec((rows, rank), lambda b, h, i: (b * n_steps + i, 0)),
                pl.BlockSpec((rank, dk), lambda b, h, i: (0, h)),
                pl.BlockSpec((1, dk), lambda b, h, i: (0, h)),
                pl.BlockSpec((1, dv), lambda b, h, i: (0, 0)),
                pl.BlockSpec((rows, dv), row_map(0))]
    args = [qkv, qkv, qkv, glr, w_a2, b_a.reshape(1, -1), gain.reshape(1, dv), gates]
    if has_state:
        in_specs.append(pl.BlockSpec((None, None, dk, dv), lambda b, h, i: (b, h, 0, 0)))
        args.append(s0)
    return pl.pallas_call(
        functools.partial(_gla_kernel, chunk=chunk, n_chunks=rows // chunk, has_state=has_state),
        grid=(batch, heads, n_steps),
        in_specs=in_specs,
        out_specs=[pl.BlockSpec((rows, dv), row_map(0)),
                   pl.BlockSpec((None, None, dk, dv), lambda b, h, i: (b, h, 0, 0))],
        out_shape=[jax.ShapeDtypeStruct((batch * length, heads * dv), BF16),
                   jax.ShapeDtypeStruct((batch, heads, dk, dv), F32)],
        scratch_shapes=[pltpu.VMEM((dv, dk), F32), pltpu.VMEM((chunk, chunk), F32),
                        pltpu.VMEM((chunk, dk), F32), pltpu.VMEM((chunk, dk), F32), pltpu.VMEM((chunk, dk), F32)],
        compiler_params=_params("parallel", "parallel", "arbitrary"),
        name="gla",
    )(*args)


def _out_kernel(oa_ref, ob_ref, sa_ref, sb_ref, x_ref, woa_ref, wob_ref, wout_ref, g_ref, y_ref):
    a = jnp.dot(oa_ref[...], woa_ref[...], preferred_element_type=F32)
    b = jnp.dot(ob_ref[...], wob_ref[...], preferred_element_type=F32)
    merged = sa_ref[...].astype(F32) * a + sb_ref[...].astype(F32) * b
    z = jnp.dot(merged.astype(BF16), wout_ref[...], preferred_element_type=F32)
    zn = z * lax.rsqrt(jnp.mean(z * z, axis=-1, keepdims=True) + NORM_EPS) * g_ref[...]
    y_ref[...] = x_ref[...] + zn


def out_proj(o_a, o_b, merge_gates, x2d, w_oa, w_ob, w_out, post_gain):
    m, d = x2d.shape
    wa, wb = o_a.shape[1], o_b.shape[1]
    tm = _tile(m, 256)
    resident = lambda shape: pl.BlockSpec(shape, lambda i: (0, 0), pipeline_mode=pl.Buffered(1))
    return pl.pallas_call(
        _out_kernel,
        grid=(m // tm,),
        in_specs=[pl.BlockSpec((tm, wa), lambda i: (i, 0)),
                  pl.BlockSpec((tm, wb), lambda i: (i, 0)),
                  pl.BlockSpec((tm, d), lambda i: (i, 0)),
                  pl.BlockSpec((tm, d), lambda i: (i, 1)),
                  pl.BlockSpec((tm, d), lambda i: (i, 0)),
                  resident((wa, d)), resident((wb, d)), resident((d, d)), resident((1, d))],
        out_specs=pl.BlockSpec((tm, d), lambda i: (i, 0)),
        out_shape=jax.ShapeDtypeStruct((m, d), F32),
        compiler_params=_params("parallel"),
        name="out_proj",
    )(o_a, o_b, merge_gates, merge_gates, x2d, w_oa, w_ob, w_out, post_gain.reshape(1, d))


def _sequence_mix(x2d, q, k16, v16, logf, glr, qkv_gla, gate_f, gate_g, merge_gates, batch, length, dims,
                  w_a2, b_a, gla_gain, w_oa, w_ob, w_out, post_g, past):
    fox_heads, head_dim, gla_heads, dk, dv = dims
    logf3 = logf.reshape(batch, length, fox_heads)
    if past is None:
        c_t = cumsum_lanes(jnp.transpose(logf3, (0, 2, 1)))
        c = jnp.transpose(c_t, (0, 2, 1)).reshape(batch * length, fox_heads)
        k_aug = key_augment(k16, c, fox_heads, head_dim)
        o_a = fox_prompt(q, k_aug, v16, c_t, gate_f, batch, length, fox_heads, head_dim)
        s0 = None
    else:
        cache_k, cache_v, cache_logf, s0, layer = past
        n_past = cache_logf.shape[1]
        logf_all = jnp.concatenate([cache_logf.astype(F32), logf3], axis=1)
        c_t = cumsum_lanes(jnp.transpose(logf_all, (0, 2, 1)))
        c = jnp.transpose(c_t[:, :, n_past:], (0, 2, 1)).reshape(batch * length, fox_heads)
        o_a = fox_cached(q, cache_k, cache_v, layer, k16, v16, c_t, c, gate_f, batch, length, fox_heads, head_dim)
    o_b, s_new = gla(qkv_gla, glr, w_a2, b_a, gla_gain, gate_g, s0, batch, length, gla_heads, dk, dv)
    y = out_proj(o_a, o_b, merge_gates, x2d, w_oa, w_ob, w_out, post_g)
    return y, logf3, s_new


def kernel(x_prompt, x_sample, cache_k, cache_v, cache_logf, state_gla, w_in, w_a2, b_a, b_f, gla_gain, w_oa, w_ob, w_out, pre_norm, post_norm):
    depth, d_model, n_in = w_in.shape
    bp, lp, _ = x_prompt.shape
    bs, ls, _ = x_sample.shape
    n_past, fox_heads, head_dim = cache_k.shape[2], cache_k.shape[3], cache_k.shape[4]
    gla_heads, dk, dv = state_gla.shape[2], state_gla.shape[3], state_gla.shape[4]
    rank = w_a2.shape[1]
    dims = (fox_heads, head_dim, gla_heads, dk, dv)
    fw, gw = fox_heads * head_dim, gla_heads * dk
    sizes = (fw, fw, fw, fox_heads, fw, gw, gw, gw, rank, gw, d_model, d_model)
    offs = [0]
    for s in sizes:
        offs.append(offs[-1] + s)
    o_fq, o_fk, o_fv, o_ff, o_fg, o_gq, _, _, o_glr, o_gg, o_ma, _, _ = offs
    assert offs[-1] == n_in and gla_heads * dv == gw

    full = (n_in // LANES) * LANES
    w_tail = jnp.pad(w_in[:, :, full:], ((0, 0), (0, 0), (0, LANES - (n_in - full))))
    cache_k2 = cache_k.reshape(depth, bs, n_past * fox_heads, head_dim)
    cache_v2 = cache_v.reshape(depth, bs, n_past * fox_heads, head_dim)
    w_oa16, w_ob16, w_out16 = w_oa.astype(BF16), w_ob.astype(BF16), w_out.astype(BF16)

    yp = x_prompt.reshape(bp * lp, d_model)
    ys = x_sample.reshape(bs * ls, d_model)
    outs_p, outs_s = [], []
    for l in range(depth):
        hp = rmsnorm_bf16(yp, pre_norm[l])
        hs = rmsnorm_bf16(ys, pre_norm[l])
        P = functools.partial(proj, hp, hs, w_in, w_tail, l)
        (q_p,), (q_s,) = P(o_fq, fw, fw, "scale", head_dim ** -0.5 * LOG2E)
        (k32_p, k16_p), (k32_s, k16_s) = P(o_fk, fw, fw, "dual")
        (v32_p, v16_p), (v32_s, v16_s) = P(o_fv, fw, fw, "dual")
        (gf_p,), (gf_s,) = P(o_fg, fw, fw, "silu")
        (gla_p,), (gla_s,) = P(o_gq, 3 * gw, gw, "scale_first", dk ** -0.5)
        (gg_p,), (gg_s,) = P(o_gg, gw, gw, "silu")
        (mg_p,), (mg_s,) = P(o_ma, 2 * d_model, d_model, "sigmoid")
        logf_p, glr_p = proj_small(hp, w_in, l, o_ff, o_glr, rank, b_f[l])
        logf_s, glr_s = proj_small(hs, w_in, l, o_ff, o_glr, rank, b_f[l])
        shared = (dims, w_a2[l], b_a[l], gla_gain[l], w_oa16[l], w_ob16[l], w_out16[l], post_norm[l])
        yp, logf3_p, s_p = _sequence_mix(yp, q_p, k16_p, v16_p, logf_p, glr_p, gla_p, gf_p, gg_p, mg_p, bp, lp,
                                         *shared, past=None)
        ys, logf3_s, s_s = _sequence_mix(ys, q_s, k16_s, v16_s, logf_s, glr_s, gla_s, gf_s, gg_s, mg_s, bs, ls,
                                         *shared, past=(cache_k2, cache_v2, cache_logf[l], state_gla[l], l))
        kv_p, kv_s = (bp, lp, fox_heads, head_dim), (bs, ls, fox_heads, head_dim)
        outs_p.append((k32_p.reshape(kv_p), v32_p.reshape(kv_p), logf3_p, s_p))
        outs_s.append((k32_s.reshape(kv_s), v32_s.reshape(kv_s), logf3_s, s_s))
    stack = lambda outs, i: jnp.stack([o[i] for o in outs])
    return (yp.reshape(bp, lp, d_model), ys.reshape(bs, ls, d_model),
            stack(outs_p, 0), stack(outs_p, 1), stack(outs_p, 2), stack(outs_p, 3),
            stack(outs_s, 0), stack(outs_s, 1), stack(outs_s, 2), stack(outs_s, 3))
```
